```python
import jax
import jax.numpy as jnp
from jax import lax
import numpy as np

D_MODEL = 2048
BATCH = 1
SEQ = 8192
DEPTH = 2
DEC_BATCH = 128
DEC_SEQ = 8
PAST_LEN = 2048
PAGE_SIZE = 128

EPS = 1e-6
N_BRANCH = 4
BRANCH_W = D_MODEL // 4
RET_HEADS = 4
RET_DK = BRANCH_W // RET_HEADS
RET_DV = BRANCH_W // RET_HEADS
RET_CHUNK = 128
ROPE_BASE = 10000.0
SB_HEADS = 4
SB_DH = BRANCH_W // SB_HEADS
SB_BLOCK = 128
CONV_CH = BRANCH_W
CONV_WIDTH = 31
GLA_HEADS = 4
GLA_DK = BRANCH_W // (2 * GLA_HEADS)
GLA_DV = BRANCH_W // GLA_HEADS
GLA_RANK = 16
GLA_TAU = 16.0
GLA_CHUNK = 64
D_FF = ((8 * D_MODEL // 3 + 127) // 128) * 128
FFN_CONV_WIDTH = 3
N_ADA = 6

IN_SIZES = (RET_HEADS * RET_DK, RET_HEADS * RET_DK, RET_HEADS * RET_DV, RET_HEADS * RET_DV,
            SB_HEADS * SB_DH, SB_HEADS * SB_DH, SB_HEADS * SB_DH,
            CONV_CH, CONV_CH,
            GLA_HEADS * GLA_DK, GLA_HEADS * GLA_DK, GLA_HEADS * GLA_DV, GLA_HEADS * GLA_DV, GLA_RANK,
            N_BRANCH * D_MODEL)
N_IN = sum(IN_SIZES)
IN_SPLITS = tuple(int(s) for s in np.cumsum(IN_SIZES)[:-1])

kernel_name = 'hybrid_retention_stickbreak_conformer_gla_decoder_step'

f32 = jnp.float32


def _chunk(L, pref):
    return pref if L % pref == 0 else L


def rms_norm(x, g):
    xf = x.astype(f32)
    y = xf * lax.rsqrt(jnp.mean(xf * xf, axis=-1, keepdims=True) + EPS)
    return y * g.astype(f32)


def modulate(x, g, shift, scale):
    y = rms_norm(x, g) * (1.0 + scale.astype(f32)[:, None, :]) + shift.astype(f32)[:, None, :]
    return y.astype(x.dtype)


def layer_norm(x, g, b):
    xf = x.astype(f32)
    mu = jnp.mean(xf, axis=-1, keepdims=True)
    xc = xf - mu
    y = xc * lax.rsqrt(jnp.mean(xc * xc, axis=-1, keepdims=True) + EPS)
    return (y * g.astype(f32) + b.astype(f32)).astype(x.dtype)


def head_norm(o, center):
    if center:
        o = o - jnp.mean(o, axis=-1, keepdims=True)
    return o * lax.rsqrt(jnp.mean(o * o, axis=-1, keepdims=True) + EPS)


def rope(x, pos):
    half = x.shape[-1] // 2
    inv = ROPE_BASE ** (-jnp.arange(half, dtype=f32) / half)
    ang = pos.astype(f32)[:, None] * inv[None, :]
    cos = jnp.cos(ang)[None, :, None, :]
    sin = jnp.sin(ang)[None, :, None, :]
    xf = x.astype(f32)
    x1, x2 = xf[..., :half], xf[..., half:]
    return jnp.concatenate([x1 * cos - x2 * sin, x1 * sin + x2 * cos], axis=-1)


def causal_dwconv(x, buf, w, b):
    W, C = w.shape
    xp = jnp.concatenate([buf.astype(x.dtype), x], axis=1)
    y = lax.conv_general_dilated(xp, w[:, None, :].astype(x.dtype), window_strides=(1,), padding='VALID',
                                 dimension_numbers=('NWC', 'WIO', 'NWC'), feature_group_count=C)
    return y + b.astype(x.dtype), xp[:, xp.shape[1] - (W - 1):]


def retention(q, k, v, s0):
    B, L, H, dk = q.shape
    C = _chunk(L, RET_CHUNK)
    n = L // C
    lg = jnp.log1p(-jnp.exp2(-5.0 - jnp.arange(H, dtype=f32)))
    q = q.astype(f32)
    k = k.astype(f32) * (dk ** -0.5)
    v = v.astype(f32)
    to_chunks = lambda t: t.reshape(B, n, C, H, t.shape[-1]).transpose(1, 0, 2, 3, 4)
    idx = jnp.arange(C, dtype=f32)
    rel = idx[:, None] - idx[None, :]
    dmask = jnp.where(rel[None] >= 0, jnp.exp(jnp.maximum(rel, 0.0)[None] * lg[:, None, None]), 0.0)
    q_dec = jnp.exp((idx[:, None] + 1.0) * lg[None, :])[None, :, :, None]
    k_dec = jnp.exp((C - 1.0 - idx)[:, None] * lg[None, :])[None, :, :, None]
    c_dec = jnp.exp(C * lg)[None, :, None, None]

    def step(S, inp):
        qc, kc, vc = inp
        att = jnp.einsum('bthd,bshd->bhts', qc, kc) * dmask[None]
        o = jnp.einsum('bhts,bshe->bthe', att, vc) + jnp.einsum('bthd,bhde->bthe', qc * q_dec, S)
        S = c_dec * S + jnp.einsum('bshd,bshe->bhde', kc * k_dec, vc)
        return S, o

    S, o = lax.scan(step, s0.astype(f32), (to_chunks(q), to_chunks(k), to_chunks(v)))
    return o.transpose(1, 0, 2, 3, 4).reshape(B, L, H, -1), S


def gla(q, k, v, log_a, s0):
    B, L, H, dk = q.shape
    C = _chunk(L, GLA_CHUNK)
    n = L // C
    q = q.astype(f32) * (dk ** -0.5)
    k = k.astype(f32)
    v = v.astype(f32)
    log_a = log_a.astype(f32)
    to_chunks = lambda t: t.reshape(B, n, C, H, t.shape[-1]).transpose(1, 0, 2, 3, 4)
    tri = (jnp.arange(C)[:, None] >= jnp.arange(C)[None, :])[None, :, :, None, None]

    def step(S, inp):
        qc, kc, vc, ac = inp
        bc = jnp.cumsum(ac, axis=1)
        diff = bc[:, :, None] - bc[:, None, :]
        E = jnp.where(tri, jnp.exp(jnp.minimum(diff, 0.0)), 0.0)
        att = jnp.sum(qc[:, :, None] * kc[:, None] * E, axis=-1)
        o = jnp.einsum('btsh,bshe->bthe', att, vc) + jnp.einsum('bthd,bhde->bthe', qc * jnp.exp(bc), S)
        blast = bc[:, -1]
        S = jnp.exp(blast)[..., None] * S + jnp.einsum('bshd,bshe->bhde', kc * jnp.exp(blast[:, None] - bc), vc)
        return S, o

    S, o = lax.scan(step, s0.astype(f32), (to_chunks(q), to_chunks(k), to_chunks(v), to_chunks(log_a)))
    return o.transpose(1, 0, 2, 3, 4).reshape(B, L, H, -1), S


def stick_breaking(q, k_all, v_all, q_pos, bias):
    B, Lq, H, d = q.shape
    Lk = k_all.shape[1]
    blk = _chunk(Lq, SB_BLOCK)
    n = Lq // blk
    kf = k_all.astype(f32)
    vf = v_all.astype(f32)
    k_pos = jnp.arange(Lk)
    scale = d ** -0.5
    bf = bias.astype(f32)[None, :, None, None]

    def one_block(inp):
        qb, qpb = inp
        z = jnp.einsum('bqhd,bkhd->bhqk', qb.astype(f32), kf) * scale + bf
        valid = (k_pos[None, :] < qpb[:, None])[None, None]
        log1m = jnp.where(valid, jax.nn.log_sigmoid(-z), 0.0)
        after = lax.cumsum(log1m, axis=3, reverse=True) - log1m
        w = jnp.where(valid, jnp.exp(jax.nn.log_sigmoid(z) + after), 0.0)
        return jnp.einsum('bhqk,bkhd->bqhd', w, vf)

    qb = q.reshape(B, n, blk, H, d).transpose(1, 0, 2, 3, 4)
    o = lax.map(one_block, (qb, q_pos.reshape(n, blk)))
    return o.transpose(1, 0, 2, 3, 4).reshape(B, Lq, H, d)


def decoder_layer(x, c, pos, k_past, v_past, ret_s, conv_buf, gla_s, ffn_buf,
                  norm1_g, norm2_g, w_ada, b_ada, w_in, gla_w_lr, gla_b_lr, sb_bias,
                  conv_w, conv_b, conv_ln_g, conv_ln_b, w_branch, w_out,
                  w_up, ffn_conv_w, ffn_conv_b, w_down):
    B, L, _ = x.shape
    dt = x.dtype
    mod = jnp.einsum('bd,de->be', jax.nn.silu(c), w_ada) + b_ada
    sh1, sc1, g1, sh2, sc2, g2 = jnp.split(mod, N_ADA, axis=-1)

    h = modulate(x, norm1_g, sh1, sc1)
    (rq, rk, rv, rg, sq, sk, sv, ca, cb, gq, gk, gv, gg, glr, gate_logits) = jnp.split(h @ w_in, IN_SPLITS, axis=-1)
    heads = lambda t, nh: t.reshape(B, L, nh, -1)

    o_ret, ret_new = retention(rope(heads(rq, RET_HEADS), pos), rope(heads(rk, RET_HEADS), pos),
                               heads(rv, RET_HEADS), ret_s)
    o_a = head_norm(o_ret, True) * jax.nn.silu(heads(rg, RET_HEADS).astype(f32))

    sk = heads(sk, SB_HEADS)
    sv = heads(sv, SB_HEADS)
    k_all = jnp.concatenate([k_past.astype(dt), sk], axis=1)
    v_all = jnp.concatenate([v_past.astype(dt), sv], axis=1)
    o_b = stick_breaking(heads(sq, SB_HEADS), k_all, v_all, pos, sb_bias)

    glu = ca * jax.nn.sigmoid(cb)
    y_c, conv_new = causal_dwconv(glu, conv_buf, conv_w, conv_b)
    o_c = jax.nn.silu(layer_norm(y_c, conv_ln_g, conv_ln_b))

    log_a = jax.nn.log_sigmoid((glr @ gla_w_lr + gla_b_lr).astype(f32)) / GLA_TAU
    o_gla, gla_new = gla(heads(gq, GLA_HEADS), heads(gk, GLA_HEADS), heads(gv, GLA_HEADS),
                         heads(log_a, GLA_HEADS), gla_s)
    o_d = head_norm(o_gla, False) * jax.nn.silu(heads(gg, GLA_HEADS).astype(f32))

    branches = (o_a.reshape(B, L, -1), o_b.reshape(B, L, -1), o_c, o_d.reshape(B, L, -1))
    gates = jax.nn.sigmoid(gate_logits).reshape(B, L, N_BRANCH, D_MODEL)
    merged = gates[:, :, 0] * (branches[0].astype(dt) @ w_branch[0])
    for i in range(1, N_BRANCH):
        merged = merged + gates[:, :, i] * (branches[i].astype(dt) @ w_branch[i])
    x = x + g1[:, None, :] * (merged @ w_out)

    h2 = modulate(x, norm2_g, sh2, sc2)
    a, b = jnp.split(h2 @ w_up, 2, axis=-1)
    a, ffn_new = causal_dwconv(a, ffn_buf, ffn_conv_w, ffn_conv_b)
    x = x + g2[:, None, :] * ((jax.nn.silu(a) * b) @ w_down)
    return x, (sk, sv, ret_new.astype(dt), conv_new, gla_new.astype(dt), ffn_new)


def setup_inputs(seed: int = 0) -> dict:
    key = jax.random.key(seed)
    ks = iter(list(jax.random.split(key, 40)))

    def nrm(shape, scale):
        return jax.random.normal(next(ks), shape, f32) * scale

    n_pages = PAST_LEN // PAGE_SIZE
    n_used = DEC_BATCH * n_pages
    n_pool = n_used + max(1, n_used // 4)
    perm = jax.random.permutation(next(ks), n_pool)
    page_table = perm[:n_used].reshape(DEC_BATCH, n_pages).astype(jnp.int32)
    D = D_MODEL
    sb_bias0 = -(5.0 + 1.5 * jnp.arange(SB_HEADS, dtype=f32))
    return {
        'x_prompt': nrm((BATCH, SEQ, D), 1.0),
        'x_sample': nrm((DEC_BATCH, DEC_SEQ, D), 1.0),
        'cache_sb_k': nrm((DEPTH, n_pool, PAGE_SIZE, SB_HEADS, SB_DH), 1.0),
        'cache_sb_v': nrm((DEPTH, n_pool, PAGE_SIZE, SB_HEADS, SB_DH), 1.0),
        'page_table': page_table,
        'state_ret': nrm((DEPTH, DEC_BATCH, RET_HEADS, RET_DK, RET_DV), 1.0),
        'state_conv': nrm((DEPTH, DEC_BATCH, CONV_WIDTH - 1, CONV_CH), 0.5),
        'state_gla': nrm((DEPTH, DEC_BATCH, GLA_HEADS, GLA_DK, GLA_DV), 1.0),
        'state_ffn_conv': nrm((DEPTH, DEC_BATCH, FFN_CONV_WIDTH - 1, D_FF), 1.0),
        'c_prompt': nrm((BATCH, D), 1.0),
        'c_sample': nrm((DEC_BATCH, D), 1.0),
        'norm1_g': 1.0 + nrm((DEPTH, D), 0.02),
        'norm2_g': 1.0 + nrm((DEPTH, D), 0.02),
        'w_ada': nrm((DEPTH, D, N_ADA * D), 0.5 * D ** -0.5),
        'b_ada': nrm((DEPTH, N_ADA * D), 0.02),
        'w_in': nrm((DEPTH, D, N_IN), D ** -0.5),
        'gla_w_lr': nrm((DEPTH, GLA_RANK, GLA_HEADS * GLA_DK), GLA_RANK ** -0.5),
        'gla_b_lr': nrm((DEPTH, GLA_HEADS * GLA_DK), 0.02),
        'sb_bias': sb_bias0[None, :] + nrm((DEPTH, SB_HEADS), 0.1),
        'conv_w': nrm((DEPTH, CONV_WIDTH, CONV_CH), CONV_WIDTH ** -0.5),
        'conv_b': nrm((DEPTH, CONV_CH), 0.02),
        'conv_ln_g': 1.0 + nrm((DEPTH, CONV_CH), 0.02),
        'conv_ln_b': nrm((DEPTH, CONV_CH), 0.02),
        'w_branch': nrm((DEPTH, N_BRANCH, BRANCH_W, D), BRANCH_W ** -0.5),
        'w_out': nrm((DEPTH, D, D), D ** -0.5),
        'w_up': nrm((DEPTH, D, 2 * D_FF), D ** -0.5),
        'ffn_conv_w': nrm((DEPTH, FFN_CONV_WIDTH, D_FF), FFN_CONV_WIDTH ** -0.5),
        'ffn_conv_b': nrm((DEPTH, D_FF), 0.02),
        'w_down': nrm((DEPTH, D_FF, D), D_FF ** -0.5),
        'final_g': 1.0 + nrm((D,), 0.02),
    }


def reference(x_prompt, x_sample, cache_sb_k, cache_sb_v, page_table, state_ret, state_conv, state_gla,
              state_ffn_conv, c_prompt, c_sample, norm1_g, norm2_g, w_ada, b_ada, w_in, gla_w_lr, gla_b_lr,
              sb_bias, conv_w, conv_b, conv_ln_g, conv_ln_b, w_branch, w_out, w_up, ffn_conv_w, ffn_conv_b,
              w_down, final_g):
    Bp, Lp, _ = x_prompt.shape
    Bs, Ls, _ = x_sample.shape
    dtp = x_prompt.dtype
    pos_p = jnp.arange(Lp)
    pos_s = PAST_LEN + jnp.arange(Ls)
    xp, xs = x_prompt, x_sample
    out_p = [[] for _ in range(6)]
    out_s = [[] for _ in range(6)]
    for l in range(DEPTH):
        lw = (norm1_g[l], norm2_g[l], w_ada[l], b_ada[l], w_in[l], gla_w_lr[l], gla_b_lr[l], sb_bias[l],
              conv_w[l], conv_b[l], conv_ln_g[l], conv_ln_b[l], w_branch[l], w_out[l],
              w_up[l], ffn_conv_w[l], ffn_conv_b[l], w_down[l])
        xp, st_p = decoder_layer(
            xp, c_prompt, pos_p,
            jnp.zeros((Bp, 0, SB_HEADS, SB_DH), dtp), jnp.zeros((Bp, 0, SB_HEADS, SB_DH), dtp),
            jnp.zeros((Bp, RET_HEADS, RET_DK, RET_DV), f32),
            jnp.zeros((Bp, CONV_WIDTH - 1, CONV_CH), dtp),
            jnp.zeros((Bp, GLA_HEADS, GLA_DK, GLA_DV), f32),
            jnp.zeros((Bp, FFN_CONV_WIDTH - 1, D_FF), dtp),
            *lw)
        k_past = cache_sb_k[l][page_table].reshape(Bs, PAST_LEN, SB_HEADS, SB_DH)
        v_past = cache_sb_v[l][page_table].reshape(Bs, PAST_LEN, SB_HEADS, SB_DH)
        xs, st_s = decoder_layer(
            xs, c_sample, pos_s, k_past, v_past,
            state_ret[l], state_conv[l], state_gla[l], state_ffn_conv[l],
            *lw)
        for i in range(6):
            out_p[i].append(st_p[i])
            out_s[i].append(st_s[i])
    y_prompt = rms_norm(xp, final_g).astype(xp.dtype)
    y_sample = rms_norm(xs, final_g).astype(xs.dtype)
    kp, vp, rp, cp, gp, fp = [jnp.stack(t, axis=0) for t in out_p]
    ks_, vs_, rs_, cs_, gs_, fs_ = [jnp.stack(t, axis=0) for t in out_s]
    return (y_prompt, y_sample, kp, vp, rp, cp, gp, fp, ks_, vs_, rs_, cs_, gs_, fs_)
```

```python
import functools

import numpy as np
import jax
import jax.numpy as jnp
from jax import lax
from jax.experimental import pallas as pl
from jax.experimental.pallas import tpu as pltpu

F32 = jnp.float32
BF16 = jnp.bfloat16

EPS = 1e-6
N_HEADS = 4
HEAD_DIM = 128
GLA_DK = 64
GLA_RANK = 16
GLA_TAU = 16.0
GLA_SUB = 16
ROPE_BASE = 10000.0
PAGE = 128
CONV_W = 31
FFN_CONV_W = 3
N_ADA = 6
LANE = 128
VMEM_LIMIT = 56 * 1024 * 1024

C_GATE = 0


def _layout(d):
    bw = d // 4
    off = {}
    o = 4 * d
    for name, w in (("rq", bw), ("rk", bw), ("rv", bw), ("rg", bw),
                    ("sq", bw), ("sk", bw), ("sv", bw),
                    ("ca", bw), ("cb", bw),
                    ("gq", bw // 2), ("gk", bw // 2), ("gv", bw), ("gg", bw),
                    ("glr", 2 * LANE)):
        off[name] = o
        o += w
    off["total"] = o
    return off


def _cparams(n_axes):
    return pltpu.CompilerParams(dimension_semantics=("arbitrary",) * n_axes,
                                vmem_limit_bytes=VMEM_LIMIT)


def _silu(x):
    return x * jax.nn.sigmoid(x)


def _softplus(x):
    return jnp.maximum(x, 0.0) + jnp.log(1.0 + jnp.exp(-jnp.abs(x)))


def _split_bf16(x):
    hi = x.astype(BF16)
    lo = (x - hi.astype(F32)).astype(BF16)
    return hi, lo


def _dot(a, b):
    return jnp.dot(a, b, preferred_element_type=F32)


def _dot_nt(a, b):
    return lax.dot_general(a, b, (((1,), (1,)), ((), ())), preferred_element_type=F32)


def _dot_tn(a, b):
    return lax.dot_general(a, b, (((0,), (0,)), ((), ())), preferred_element_type=F32)


def _ada_body(c_ref, w_ref, b_ref, o_ref):
    s = _silu(c_ref[...]).astype(BF16)
    o_ref[0] = _dot(s, w_ref[0].astype(BF16)) + b_ref[0]


def _ada(c_all, w_ada, b_ada):
    depth, d, n = w_ada.shape
    m = c_all.shape[0]
    tn = 1024
    return pl.pallas_call(
        _ada_body,
        grid=(depth, n // tn),
        in_specs=[pl.BlockSpec((m, d), lambda l, j: (0, 0)),
                  pl.BlockSpec((1, d, tn), lambda l, j: (l, 0, j)),
                  pl.BlockSpec((1, 1, tn), lambda l, j: (l, 0, j))],
        out_specs=pl.BlockSpec((1, m, tn), lambda l, j: (l, 0, j)),
        out_shape=jax.ShapeDtypeStruct((depth, m, n), F32),
        compiler_params=_cparams(2),
        name="ada_mod",
    )(c_all, w_ada, b_ada.reshape(depth, 1, n))


def _modulate(x, g, sh, sc):
    ms = jnp.mean(x * x, axis=-1, keepdims=True)
    y = x * lax.rsqrt(ms + EPS) * g
    return y * (1.0 + sc) + sh


def _modproj_body(x_ref, g_ref, sh_ref, sc_ref, w_ref, o_ref, h_ref):
    @pl.when(pl.program_id(1) == 0)
    def _():
        h = _modulate(x_ref[...], g_ref[...], sh_ref[...], sc_ref[...])
        h_ref[...] = h.reshape(h_ref.shape).astype(BF16)

    o_ref[...] = _dot(h_ref[...], w_ref[...])


def _row_tiling(b, l, tm):
    if b == 1:
        r = min(tm, l)
        return 1, r, l // r, (lambda i: (0, i, 0))
    g = min(max(tm // l, 1), b)
    return g, l, b // g, (lambda i: (i, 0, 0))


def _modproj(x, g, sh, sc, w, tm, tn):
    b, l, d = x.shape
    n = w.shape[1]
    gg, r, nt, xmap = _row_tiling(b, l, tm)
    gmap = (lambda i: (0, 0, 0)) if b == 1 else (lambda i: (i, 0, 0))
    return pl.pallas_call(
        _modproj_body,
        grid=(nt, n // tn),
        in_specs=[pl.BlockSpec((gg, r, d), lambda i, j: xmap(i)),
                  pl.BlockSpec((1, d), lambda i, j: (0, 0)),
                  pl.BlockSpec((gg, 1, d), lambda i, j: gmap(i)),
                  pl.BlockSpec((gg, 1, d), lambda i, j: gmap(i)),
                  pl.BlockSpec((d, tn), lambda i, j: (0, j))],
        out_specs=pl.BlockSpec((gg * r, tn), lambda i, j: (i, j)),
        out_shape=jax.ShapeDtypeStruct((b * l, n), F32),
        scratch_shapes=[pltpu.VMEM((gg * r, d), BF16)],
        compiler_params=_cparams(2),
        name="modproj",
    )(x, g.reshape(1, d), sh, sc, w)


def _ret_body(*refs, c, n, has_s0, cdec):
    if has_s0:
        (rq, rk, rv, rg, cos, sin, dm, qd, kd, s0, o_ref, sn_ref, s_scr, kp, vp) = refs
    else:
        (rq, rk, rv, rg, cos, sin, dm, qd, kd, o_ref, sn_ref, s_scr, kp, vp) = refs
        s0 = None
    i = pl.program_id(1)

    @pl.when(i == 0)
    def _():
        if has_s0:
            s_scr[...] = s0[0]
        else:
            s_scr[...] = jnp.zeros(s_scr.shape, F32)
        if c < LANE:
            kp[...] = jnp.zeros(kp.shape, F32)
            vp[...] = jnp.zeros(vp.shape, F32)

    c2 = cos[...]
    s2 = sin[...]
    for h in range(N_HEADS):
        sl = slice(HEAD_DIM * h, HEAD_DIM * (h + 1))
        q = rq[:, sl]
        k = rk[:, sl]
        v = rv[:, sl]
        gt = rg[:, sl]
        qr = q * c2 + pltpu.roll(q, HEAD_DIM // 2, 1) * s2
        kr = (k * c2 + pltpu.roll(k, HEAD_DIM // 2, 1) * s2) * (HEAD_DIM ** -0.5)
        if c < LANE:
            kp[0, 0:c, sl] = kr
            kp[1, 0:c, sl] = kr * kd[h]
            vp[0:c, sl] = v
            kb = kp[0, :, sl].astype(BF16)
            kdb = kp[1, :, sl].astype(BF16)
            vb = vp[:, sl].astype(BF16)
        else:
            kb = kr.astype(BF16)
            vb = v.astype(BF16)
            kdb = (kr * kd[h]).astype(BF16)
        att = _dot_nt(qr.astype(BF16), kb) * dm[h]
        sh_ = s_scr[h]
        o = _dot(att.astype(BF16), vb) + _dot((qr * qd[h]).astype(BF16), sh_.astype(BF16))
        s_scr[h] = cdec[h] * sh_ + _dot_tn(kdb, vb)
        o = o - jnp.mean(o, axis=-1, keepdims=True)
        o = o * lax.rsqrt(jnp.mean(o * o, axis=-1, keepdims=True) + EPS)
        o_ref[:, sl] = (o * _silu(gt)).astype(o_ref.dtype)

    @pl.when(i == n - 1)
    def _():
        sn_ref[0] = s_scr[...]


def _rope_tables(pos):
    half = HEAD_DIM // 2
    inv = ROPE_BASE ** (-jnp.arange(half, dtype=F32) / half)
    ang = pos.astype(F32)[:, None] * inv[None, :]
    cos = jnp.cos(ang)
    sin = jnp.sin(ang)
    return jnp.concatenate([cos, cos], axis=-1), jnp.concatenate([-sin, sin], axis=-1)


def _ret_consts(c):
    ck = max(c, LANE)
    lg = np.log1p(-np.exp2(-5.0 - np.arange(N_HEADS, dtype=np.float64)))
    idx = np.arange(c, dtype=np.float64)
    rel = idx[:, None] - idx[None, :]
    dmask = np.where(rel[None] >= 0, np.exp(np.maximum(rel, 0.0)[None] * lg[:, None, None]), 0.0)
    dm = np.zeros((N_HEADS, c, ck))
    dm[:, :, :c] = dmask
    qd = np.exp((idx[None, :] + 1.0) * lg[:, None])[:, :, None] * np.ones((1, 1, HEAD_DIM))
    kd = np.exp((c - 1.0 - idx)[None, :] * lg[:, None])[:, :, None] * np.ones((1, 1, HEAD_DIM))
    cdec = tuple(float(v) for v in np.exp(c * lg))
    return (jnp.asarray(dm, F32), jnp.asarray(qd, F32), jnp.asarray(kd, F32), cdec)


def _retention(proj, off, b, l, cos, sin, s0):
    c = 128 if l % 128 == 0 else l
    n = l // c
    ck = max(c, LANE)
    dm, qd, kd, cdec = _ret_consts(c)
    bw = N_HEADS * HEAD_DIM
    cb = lambda name: off[name] // bw

    def pspec(name):
        j = cb(name)
        return pl.BlockSpec((c, bw), lambda bi, i: (bi * n + i, j))

    tab = pl.BlockSpec((c, HEAD_DIM), lambda bi, i: (i, 0))
    const3 = lambda a: pl.BlockSpec(a.shape, lambda bi, i: (0, 0, 0))
    in_specs = [pspec("rq"), pspec("rk"), pspec("rv"), pspec("rg"), tab, tab,
                const3(dm), const3(qd), const3(kd)]
    args = [proj, proj, proj, proj, cos, sin, dm, qd, kd]
    if s0 is not None:
        in_specs.append(pl.BlockSpec((1, N_HEADS, HEAD_DIM, HEAD_DIM), lambda bi, i: (bi, 0, 0, 0)))
        args.append(s0)
    return pl.pallas_call(
        functools.partial(_ret_body, c=c, n=n, has_s0=s0 is not None, cdec=cdec),
        grid=(b, n),
        in_specs=in_specs,
        out_specs=[pl.BlockSpec((c, bw), lambda bi, i: (bi * n + i, 0)),
                   pl.BlockSpec((1, N_HEADS, HEAD_DIM, HEAD_DIM), lambda bi, i: (bi, 0, 0, 0))],
        out_shape=[jax.ShapeDtypeStruct((b * l, bw), BF16),
                   jax.ShapeDtypeStruct((b, N_HEADS, HEAD_DIM, HEAD_DIM), F32)],
        scratch_shapes=[pltpu.VMEM((N_HEADS, HEAD_DIM, HEAD_DIM), F32),
                        pltpu.VMEM((2, ck, bw), F32), pltpu.VMEM((ck, bw), F32)],
        compiler_params=_cparams(2),
        name="retention",
    )(*args)


def _sb_suffix_matrix(tk):
    j = np.arange(tk)[:, None]
    s = np.arange(tk + LANE)[None, :]
    u = np.where(s < tk, j > s, True)
    return jnp.asarray(u, BF16)


def _sb_block(q, k, v, bias, valid, u, carry):
    tk = k.shape[0]
    z = _dot_nt(q, k) * (HEAD_DIM ** -0.5) + bias
    sp = _softplus(z)
    l1m = -sp
    if valid is not None:
        l1m = jnp.where(valid, l1m, 0.0)
    hi, lo = _split_bf16(l1m)
    r = _dot(hi, u) + _dot(lo, u)
    after = r[:, :tk] + jnp.concatenate([carry] * (tk // LANE), axis=1)
    w = jnp.exp(z - sp + after)
    if valid is not None:
        w = jnp.where(valid, w, 0.0)
    return _dot(w.astype(BF16), v), carry + r[:, tk:]


def _sbp_body(qb_ref, kb_ref, bias_ref, q_ref, k_ref, v_ref, u_ref, o_ref, carry, acc, *, t):
    h = pl.program_id(0)
    p = pl.program_id(1)
    qb = qb_ref[p]
    kb = kb_ref[p]

    @pl.when(kb == qb)
    def _():
        carry[...] = jnp.zeros(carry.shape, F32)
        acc[...] = jnp.zeros(acc.shape, F32)

    rows = qb * t + lax.broadcasted_iota(jnp.int32, (t, t), 0)
    cols = kb * t + lax.broadcasted_iota(jnp.int32, (t, t), 1)
    pv, cnew = _sb_block(q_ref[...].astype(BF16), k_ref[...].astype(BF16), v_ref[...].astype(BF16),
                         bias_ref[h], cols < rows, u_ref[...], carry[...])
    acc[...] += pv
    carry[...] = cnew

    @pl.when(kb == 0)
    def _():
        o_ref[...] = acc[...].astype(o_ref.dtype)


def _sb_prompt(proj, off, l, bias, t=256):
    t = min(t, l)
    nq = l // t
    qb = np.concatenate([np.full(i + 1, i) for i in range(nq)]).astype(np.int32)
    kb = np.concatenate([np.arange(i, -1, -1) for i in range(nq)]).astype(np.int32)
    cq, ck, cv = (off[k] // HEAD_DIM for k in ("sq", "sk", "sv"))
    u = _sb_suffix_matrix(t)
    grid_spec = pltpu.PrefetchScalarGridSpec(
        num_scalar_prefetch=2,
        grid=(N_HEADS, len(qb)),
        in_specs=[pl.BlockSpec(memory_space=pltpu.SMEM),
                  pl.BlockSpec((t, HEAD_DIM), lambda h, p, qb, kb: (qb[p], cq + h)),
                  pl.BlockSpec((t, HEAD_DIM), lambda h, p, qb, kb: (kb[p], ck + h)),
                  pl.BlockSpec((t, HEAD_DIM), lambda h, p, qb, kb: (kb[p], cv + h)),
                  pl.BlockSpec(u.shape, lambda h, p, qb, kb: (0, 0))],
        out_specs=pl.BlockSpec((t, HEAD_DIM), lambda h, p, qb, kb: (qb[p], h)),
        scratch_shapes=[pltpu.VMEM((t, LANE), F32), pltpu.VMEM((t, HEAD_DIM), F32)])
    return pl.pallas_call(
        functools.partial(_sbp_body, t=t),
        grid_spec=grid_spec,
        out_shape=jax.ShapeDtypeStruct((l, N_HEADS * HEAD_DIM), BF16),
        compiler_params=_cparams(2),
        name="sb_prompt",
    )(jnp.asarray(qb), jnp.asarray(kb), bias, proj, proj, proj, u)


def _sbs_body(*refs, l, n_pages, pps):
    pt_ref, bias_ref, q_ref, kn_ref, vn_ref = refs[:5]
    kpages = refs[5:5 + pps]
    vpages = refs[5 + pps:5 + 2 * pps]
    u_ref, o_ref, carry, acc, kp, vp = refs[5 + 2 * pps:]
    s = pl.program_id(1)
    n_steps = pl.num_programs(1)
    bw = N_HEADS * HEAD_DIM

    @pl.when(s == 0)
    def _():
        kp[...] = jnp.zeros(kp.shape, F32)
        vp[...] = jnp.zeros(vp.shape, F32)
        kp[0:l, :] = kn_ref[...]
        vp[0:l, :] = vn_ref[...]
        rows = lax.broadcasted_iota(jnp.int32, (l, PAGE), 0)
        cols = lax.broadcasted_iota(jnp.int32, (l, PAGE), 1)
        valid = cols < rows
        for h in range(N_HEADS):
            sl = slice(HEAD_DIM * h, HEAD_DIM * (h + 1))
            pv, cnew = _sb_block(q_ref[:, sl].astype(BF16), kp[:, sl].astype(BF16),
                                 vp[:, sl].astype(BF16), bias_ref[h], valid,
                                 u_ref[...], jnp.zeros((l, LANE), F32))
            acc[h] = pv
            carry[h] = cnew

    @pl.when(s > 0)
    def _():
        for pp in range(pps):
            for h in range(N_HEADS):
                sl = slice(HEAD_DIM * h, HEAD_DIM * (h + 1))
                pv, cnew = _sb_block(q_ref[:, sl].astype(BF16), kpages[pp][0, :, sl].astype(BF16),
                                     vpages[pp][0, :, sl].astype(BF16), bias_ref[h], None,
                                     u_ref[...], carry[h])
                acc[h] += pv
                carry[h] = cnew

    @pl.when(s == n_steps - 1)
    def _():
        for h in range(N_HEADS):
            o_ref[:, HEAD_DIM * h:HEAD_DIM * (h + 1)] = acc[h].astype(o_ref.dtype)


def _sb_sample(proj, off, b, l, bias, cache_k, cache_v, page_table, pps=4):
    n_pages = page_table.shape[1]
    pps = min(pps, n_pages)
    assert n_pages % pps == 0
    bw = N_HEADS * HEAD_DIM
    n_pool = cache_k.shape[0]
    ck = cache_k.reshape(n_pool, PAGE, bw)
    cv = cache_v.reshape(n_pool, PAGE, bw)
    u = _sb_suffix_matrix(PAGE)
    cq, ckn, cvn = (off[k] // bw for k in ("sq", "sk", "sv"))

    def page_spec(pp):
        def imap(bi, s, pt):
            page = n_pages - 1 - ((s - 1) * pps + pp)
            page = jnp.clip(page, 0, n_pages - 1)
            return (pt[bi * n_pages + page], 0, 0)
        return pl.BlockSpec((1, PAGE, bw), imap)

    in_specs = ([pl.BlockSpec(memory_space=pltpu.SMEM),
                 pl.BlockSpec((l, bw), lambda bi, s, pt: (bi, cq)),
                 pl.BlockSpec((l, bw), lambda bi, s, pt: (bi, ckn)),
                 pl.BlockSpec((l, bw), lambda bi, s, pt: (bi, cvn))]
                + [page_spec(pp) for pp in range(pps)] * 2
                + [pl.BlockSpec(u.shape, lambda bi, s, pt: (0, 0))])
    grid_spec = pltpu.PrefetchScalarGridSpec(
        num_scalar_prefetch=1,
        grid=(b, 1 + n_pages // pps),
        in_specs=in_specs,
        out_specs=pl.BlockSpec((l, bw), lambda bi, s, pt: (bi, 0)),
        scratch_shapes=[pltpu.VMEM((N_HEADS, l, LANE), F32), pltpu.VMEM((N_HEADS, l, HEAD_DIM), F32),
                        pltpu.VMEM((PAGE, bw), F32), pltpu.VMEM((PAGE, bw), F32)])
    return pl.pallas_call(
        functools.partial(_sbs_body, l=l, n_pages=n_pages, pps=pps),
        grid_spec=grid_spec,
        out_shape=jax.ShapeDtypeStruct((b * l, bw), BF16),
        compiler_params=_cparams(2),
        name="sb_sample",
    )(page_table.reshape(-1), bias, proj, proj, proj, *([ck] * pps), *([cv] * pps), u)


CONV_HALO = 32


def _conv_body(*refs, g, r, n, has_buf):
    if has_buf:
        ca, cb, buf, w_ref, b_ref, lg_ref, lb_ref, o_ref, new_ref, xp = refs
    else:
        ca, cb, w_ref, b_ref, lg_ref, lb_ref, o_ref, new_ref, xp = refs
    i = pl.program_id(1)
    nh = CONV_W - 1
    ch = ca.shape[-1]

    @pl.when(i == 0)
    def _():
        xp[:, 0:CONV_HALO, :] = jnp.zeros((g, CONV_HALO, ch), F32)
        if has_buf:
            xp[:, CONV_HALO - nh:CONV_HALO, :] = buf[...]

    glu = ca[...] * jax.nn.sigmoid(cb[...])
    xp[:, CONV_HALO:CONV_HALO + r, :] = glu.reshape(g, r, ch)
    base = CONV_HALO - nh
    y = jnp.zeros((g, r, ch), F32) + b_ref[...]
    for j in range(CONV_W):
        y = y + w_ref[j:j + 1, :] * xp[:, base + j:base + j + r, :]
    mu = jnp.mean(y, axis=-1, keepdims=True)
    yc = y - mu
    yn = yc * lax.rsqrt(jnp.mean(yc * yc, axis=-1, keepdims=True) + EPS) * lg_ref[...] + lb_ref[...]
    o_ref[...] = _silu(yn).reshape(g * r, ch).astype(o_ref.dtype)
    tail = xp[:, base + r:CONV_HALO + r, :]
    xp[:, base:CONV_HALO, :] = tail

    @pl.when(i == n - 1)
    def _():
        new_ref[...] = tail


def _conv_branch(proj, off, b, l, buf, w, bias, ln_g, ln_b, tm=512):
    ch = w.shape[1]
    g, r, nt, _ = _row_tiling(b, l, tm)
    n = nt if b == 1 else 1
    ja, jb = off["ca"] // ch, off["cb"] // ch
    rowblk = (lambda bi, i: i) if b == 1 else (lambda bi, i: bi)
    grid = (1, nt) if b == 1 else (nt, 1)
    in_specs = [pl.BlockSpec((g * r, ch), lambda bi, i: (rowblk(bi, i), ja)),
                pl.BlockSpec((g * r, ch), lambda bi, i: (rowblk(bi, i), jb))]
    args = [proj, proj]
    if buf is not None:
        in_specs.append(pl.BlockSpec((g, CONV_W - 1, ch), lambda bi, i: (bi, 0, 0)))
        args.append(buf)
    const2 = lambda a: pl.BlockSpec(a.shape, lambda bi, i: (0, 0))
    small = [w, bias.reshape(1, ch), ln_g.reshape(1, ch), ln_b.reshape(1, ch)]
    in_specs += [const2(a) for a in small]
    args += small
    return pl.pallas_call(
        functools.partial(_conv_body, g=g, r=r, n=n, has_buf=buf is not None),
        grid=grid,
        in_specs=in_specs,
        out_specs=[pl.BlockSpec((g * r, ch), lambda bi, i: (rowblk(bi, i), 0)),
                   pl.BlockSpec((g, CONV_W - 1, ch), lambda bi, i: (bi, 0, 0))],
        out_shape=[jax.ShapeDtypeStruct((b * l, ch), BF16),
                   jax.ShapeDtypeStruct((b, CONV_W - 1, ch), F32)],
        scratch_shapes=[pltpu.VMEM((g, CONV_HALO + r, ch), F32)],
        compiler_params=_cparams(2),
        name="conv_branch",
    )(*args)


def _gla_consts():
    kd = N_HEADS * GLA_DK
    sel = (np.arange(kd)[:, None] // GLA_DK) == (np.arange(N_HEADS * HEAD_DIM)[None, :] // HEAD_DIM)
    tri = np.arange(GLA_SUB)[:, None] >= np.arange(GLA_SUB)[None, :]
    return jnp.asarray(sel, BF16), jnp.asarray(tri, BF16)


def _gla_sub(q, k, v, gt, glr, wlr, blr, sel, tri, st, pad):
    c = GLA_SUB
    kd = N_HEADS * GLA_DK
    x = _dot(glr.astype(BF16), wlr) + blr
    la = -_softplus(-x) * (1.0 / GLA_TAU)
    row = lax.broadcasted_iota(jnp.int32, (c, kd), 0)
    if pad:
        la = jnp.where(row >= pad, la, 0.0)
    hi, lo = _split_bf16(la)
    bc = _dot(tri, hi) + _dot(tri, lo)
    qs = q * (GLA_DK ** -0.5)
    blocks = []
    for s in range(pad, c):
        e = jnp.where(row >= s, jnp.exp(jnp.minimum(bc - bc[s:s + 1, :], 0.0)), 0.0)
        blocks.append((qs * k[s:s + 1, :] * e).astype(BF16))
    attb = _dot(jnp.concatenate(blocks, axis=0), sel)
    od = jnp.zeros((c, N_HEADS * HEAD_DIM), F32)
    for n_, s in enumerate(range(pad, c)):
        od = od + attb[n_ * c:(n_ + 1) * c, :] * v[s:s + 1, :]
    blast = bc[c - 1:c, :]
    qe = qs * jnp.exp(bc)
    ke = k * jnp.exp(blast - bc)
    lane_head = lax.broadcasted_iota(jnp.int32, (c, kd), 1) // GLA_DK
    stb = st.astype(BF16)
    ds = jnp.zeros(st.shape, F32)
    ois = []
    for h in range(N_HEADS):
        hm = lane_head == h
        ois.append(_dot_nt(jnp.where(hm, qe, 0.0).astype(BF16), stb))
        ds = ds + _dot_tn(v[:, HEAD_DIM * h:HEAD_DIM * (h + 1)].astype(BF16),
                          jnp.where(hm, ke, 0.0).astype(BF16))
    st_new = jnp.exp(blast) * st + ds
    o = od + jnp.concatenate(ois, axis=1)
    outs = []
    for h in range(N_HEADS):
        oh = o[:, HEAD_DIM * h:HEAD_DIM * (h + 1)]
        outs.append(oh * lax.rsqrt(jnp.mean(oh * oh, axis=-1, keepdims=True) + EPS))
    return jnp.concatenate(outs, axis=1) * _silu(gt), st_new


def _gla_body(*refs, rb, n, has_s0):
    if has_s0:
        gq, gk, gv, gg, glr, wlr, blr, sel, tri, s0, o_ref, sn_ref, st = refs
    else:
        gq, gk, gv, gg, glr, wlr, blr, sel, tri, o_ref, sn_ref, st = refs
    i = pl.program_id(1)

    @pl.when(i == 0)
    def _():
        if has_s0:
            st[...] = s0[0]
        else:
            st[...] = jnp.zeros(st.shape, F32)

    c = GLA_SUB
    if rb < c:
        pad = c - rb
        zp = lambda a: jnp.concatenate([jnp.zeros((pad, a.shape[1]), F32), a], axis=0)
        o, st_new = _gla_sub(zp(gq[...]), zp(gk[...]), zp(gv[...]), zp(gg[...]), zp(glr[...]),
                             wlr[...], blr[...], sel[...], tri[...], st[...], pad)
        o_ref[...] = o[pad:, :].astype(o_ref.dtype)
        st[...] = st_new
    else:
        for m in range(rb // c):
            rs = slice(m * c, (m + 1) * c)
            o, st_new = _gla_sub(gq[rs, :], gk[rs, :], gv[rs, :], gg[rs, :], glr[rs, :],
                                 wlr[...], blr[...], sel[...], tri[...], st[...], 0)
            o_ref[rs, :] = o.astype(o_ref.dtype)
            st[...] = st_new

    @pl.when(i == n - 1)
    def _():
        sn_ref[0] = st[...]


def _gla(proj, off, b, l, wlr, blr, s0t, rb=128):
    rb = min(rb, l)
    n = l // rb
    kd = N_HEADS * GLA_DK
    bw = N_HEADS * HEAD_DIM
    sel, tri = _gla_consts()

    def pspec(name, width):
        j = off[name] // width
        return pl.BlockSpec((rb, width), lambda bi, i: (bi * n + i, j))

    const2 = lambda a: pl.BlockSpec(a.shape, lambda bi, i: (0, 0))
    in_specs = [pspec("gq", kd), pspec("gk", kd), pspec("gv", bw), pspec("gg", bw), pspec("glr", LANE),
                const2(wlr), const2(blr), const2(sel), const2(tri)]
    args = [proj, proj, proj, proj, proj, wlr, blr, sel, tri]
    if s0t is not None:
        in_specs.append(pl.BlockSpec((1, HEAD_DIM, kd), lambda bi, i: (bi, 0, 0)))
        args.append(s0t)
    return pl.pallas_call(
        functools.partial(_gla_body, rb=rb, n=n, has_s0=s0t is not None),
        grid=(b, n),
        in_specs=in_specs,
        out_specs=[pl.BlockSpec((rb, bw), lambda bi, i: (bi * n + i, 0)),
                   pl.BlockSpec((1, HEAD_DIM, kd), lambda bi, i: (bi, 0, 0))],
        out_shape=[jax.ShapeDtypeStruct((b * l, bw), BF16),
                   jax.ShapeDtypeStruct((b, HEAD_DIM, kd), F32)],
        scratch_shapes=[pltpu.VMEM((HEAD_DIM, kd), F32)],
        compiler_params=_cparams(2),
        name="gla",
    )(*args)


def _merge_body(a_ref, b_ref, c_ref, d_ref, g0, g1, g2, g3, w_ref, o_ref):
    acc = None
    for i, (br, gl) in enumerate(((a_ref, g0), (b_ref, g1), (c_ref, g2), (d_ref, g3))):
        t = jax.nn.sigmoid(gl[...]) * _dot(br[...], w_ref[i])
        acc = t if acc is None else acc + t
    o_ref[...] = acc.astype(o_ref.dtype)


def _merge(branches, proj, wb, tm=512, tn=1024):
    rows, bw = branches[0].shape
    d = wb.shape[2]
    tm = min(tm, rows)
    nj = d // tn
    br_spec = pl.BlockSpec((tm, bw), lambda i, j: (i, 0))
    gate_specs = [pl.BlockSpec((tm, tn), (lambda i, j, q=q: (i, q * nj + j))) for q in range(4)]
    return pl.pallas_call(
        _merge_body,
        grid=(rows // tm, nj),
        in_specs=[br_spec] * 4 + gate_specs + [pl.BlockSpec((4, bw, tn), lambda i, j: (0, 0, j))],
        out_specs=pl.BlockSpec((tm, tn), lambda i, j: (i, j)),
        out_shape=jax.ShapeDtypeStruct((rows, d), BF16),
        compiler_params=_cparams(2),
        name="merge",
    )(*branches, proj, proj, proj, proj, wb)


def _resid_body(a_ref, w_ref, x_ref, g_ref, o_ref):
    y = _dot(a_ref[...], w_ref[...])
    o_ref[...] = x_ref[...] + g_ref[...] * y.reshape(o_ref.shape)


def _resid_proj(a, w, x, gate, tm=512, tn=512):
    b, l, d = x.shape
    k = a.shape[1]
    g, r, nt, xmap = _row_tiling(b, l, tm)
    gmap = (lambda i: (0, 0)) if b == 1 else (lambda i: (i, 0))
    return pl.pallas_call(
        _resid_body,
        grid=(nt, d // tn),
        in_specs=[pl.BlockSpec((g * r, k), lambda i, j: (i, 0)),
                  pl.BlockSpec((k, tn), lambda i, j: (0, j)),
                  pl.BlockSpec((g, r, tn), lambda i, j: xmap(i)[:2] + (j,)),
                  pl.BlockSpec((g, 1, tn), lambda i, j: gmap(i) + (j,))],
        out_specs=pl.BlockSpec((g, r, tn), lambda i, j: xmap(i)[:2] + (j,)),
        out_shape=jax.ShapeDtypeStruct((b, l, d), F32),
        compiler_params=_cparams(2),
        name="resid_proj",
    )(a, w, x, gate)


FFN_HALO = 8


def _ffn_body(*refs, g, r, has_buf):
    if has_buf:
        (x_ref, n_ref, sh_ref, sc_ref, wa_ref, wb_ref, cw_ref, cb_ref, buf_ref,
         y_ref, new_ref, h_ref, ap, carry) = refs
    else:
        (x_ref, n_ref, sh_ref, sc_ref, wa_ref, wb_ref, cw_ref, cb_ref,
         y_ref, new_ref, h_ref, ap, carry) = refs
    i = pl.program_id(0)
    j = pl.program_id(1)
    nh = FFN_CONV_W - 1
    tn = wa_ref.shape[1]

    @pl.when(j == 0)
    def _():
        h = _modulate(x_ref[...], n_ref[...], sh_ref[...], sc_ref[...])
        h_ref[...] = h.reshape(h_ref.shape).astype(BF16)

    hb = h_ref[...]
    a3 = _dot(hb, wa_ref[...]).reshape(g, r, tn)
    b3 = _dot(hb, wb_ref[...]).reshape(g, r, tn)
    if has_buf:
        halo = buf_ref[...]
    else:
        halo = jnp.where(i == 0, 0.0, carry[j])
    ap[:, FFN_HALO - nh:FFN_HALO, :] = halo
    ap[:, FFN_HALO:FFN_HALO + r, :] = a3
    conv = (cb_ref[...] + cw_ref[0:1, :] * ap[:, FFN_HALO - 2:FFN_HALO - 2 + r, :]
            + cw_ref[1:2, :] * ap[:, FFN_HALO - 1:FFN_HALO - 1 + r, :] + cw_ref[2:3, :] * a3)
    y_ref[...] = (_silu(conv) * b3).reshape(g * r, tn).astype(y_ref.dtype)
    tail = a3[:, r - nh:r, :]
    new_ref[...] = tail
    if not has_buf:
        carry[j] = tail


def _ffn_up(x, ng, sh, sc, wa, wb, cw, cb, buf, tm=512, tn=512):
    b, l, d = x.shape
    f = wa.shape[1]
    g, r, nt, xmap = _row_tiling(b, l, tm)
    gmap = (lambda i: (0, 0, 0)) if b == 1 else (lambda i: (i, 0, 0))
    nh = FFN_CONV_W - 1
    nj = f // tn
    in_specs = [pl.BlockSpec((g, r, d), lambda i, j: xmap(i)),
                pl.BlockSpec((1, d), lambda i, j: (0, 0)),
                pl.BlockSpec((g, 1, d), lambda i, j: gmap(i)),
                pl.BlockSpec((g, 1, d), lambda i, j: gmap(i)),
                pl.BlockSpec((d, tn), lambda i, j: (0, j)),
                pl.BlockSpec((d, tn), lambda i, j: (0, j)),
                pl.BlockSpec((FFN_CONV_W, tn), lambda i, j: (0, j)),
                pl.BlockSpec((1, tn), lambda i, j: (0, j))]
    args = [x, ng.reshape(1, d), sh, sc, wa, wb, cw, cb.reshape(1, f)]
    if buf is not None:
        in_specs.append(pl.BlockSpec((g, nh, tn), lambda i, j: (i, 0, j)))
        args.append(buf)
    newmap = (lambda i, j: (0, 0, j)) if b == 1 else (lambda i, j: (i, 0, j))
    return pl.pallas_call(
        functools.partial(_ffn_body, g=g, r=r, has_buf=buf is not None),
        grid=(nt, nj),
        in_specs=in_specs,
        out_specs=[pl.BlockSpec((g * r, tn), lambda i, j: (i, j)),
                   pl.BlockSpec((g, nh, tn), newmap)],
        out_shape=[jax.ShapeDtypeStruct((b * l, f), BF16),
                   jax.ShapeDtypeStruct((b, nh, f), F32)],
        scratch_shapes=[pltpu.VMEM((g * r, d), BF16),
                        pltpu.VMEM((g, FFN_HALO + r, tn), F32),
                        pltpu.VMEM((nj, g, nh, tn), F32)],
        compiler_params=_cparams(2),
        name="ffn_up",
    )(*args)


def _final_body(x_ref, g_ref, o_ref):
    x = x_ref[...]
    o_ref[...] = x * lax.rsqrt(jnp.mean(x * x, axis=-1, keepdims=True) + EPS) * g_ref[...]


def _final_norm(x, g, tm=512):
    b, l, d = x.shape
    gg, r, nt, xmap = _row_tiling(b, l, tm)
    return pl.pallas_call(
        _final_body,
        grid=(nt,),
        in_specs=[pl.BlockSpec((gg, r, d), xmap), pl.BlockSpec((1, d), lambda i: (0, 0))],
        out_specs=pl.BlockSpec((gg, r, d), xmap),
        out_shape=jax.ShapeDtypeStruct((b, l, d), F32),
        compiler_params=_cparams(1),
        name="final_norm",
    )(x, g.reshape(1, d))


def _prep_w_in(w_in_l, d):
    bw = d // 4
    core = 4 * bw + 3 * bw + 2 * bw + (bw // 2) * 2 + 2 * bw
    glr = w_in_l[:, core:core + GLA_RANK]
    gate = w_in_l[:, core + GLA_RANK:]
    pad = jnp.zeros((d, 2 * LANE - GLA_RANK), w_in_l.dtype)
    return jnp.concatenate([gate, w_in_l[:, :core], glr, pad], axis=1).astype(BF16)


def _pad_cols(a, n):
    return jnp.pad(a, [(0, 0)] * (a.ndim - 1) + [(0, n - a.shape[-1])])


def _layer(x, mods, cos, sin, ret_s0, conv_buf, gla_s0, ffn_buf, sb_cache, lw, off):
    b, l, d = x.shape
    sh1, sc1, g1, sh2, sc2, g2 = mods
    proj = _modproj(x, lw["norm1_g"], sh1, sc1, lw["w_in"], tm=512, tn=768)

    o_a, ret_new = _retention(proj, off, b, l, cos, sin, ret_s0)
    if sb_cache is None:
        o_b = _sb_prompt(proj, off, l, lw["sb_bias"])
    else:
        o_b = _sb_sample(proj, off, b, l, lw["sb_bias"], *sb_cache)
    o_c, conv_new = _conv_branch(proj, off, b, l, conv_buf, lw["conv_w"], lw["conv_b"],
                                 lw["conv_ln_g"], lw["conv_ln_b"])
    s0t = None
    if gla_s0 is not None:
        s0t = gla_s0.transpose(0, 3, 1, 2).reshape(b, HEAD_DIM, N_HEADS * GLA_DK)
    o_d, gla_t = _gla(proj, off, b, l, lw["gla_w_lr"], lw["gla_b_lr"], s0t)
    gla_new = gla_t.reshape(b, HEAD_DIM, N_HEADS, GLA_DK).transpose(0, 2, 3, 1)

    merged = _merge((o_a, o_b, o_c, o_d), proj, lw["w_branch"])
    x1 = _resid_proj(merged, lw["w_out"], x, g1)

    y, ffn_new = _ffn_up(x1, lw["norm2_g"], sh2, sc2, lw["w_up_a"], lw["w_up_b"],
                         lw["ffn_conv_w"], lw["ffn_conv_b"], ffn_buf)
    x2 = _resid_proj(y, lw["w_down"], x1, g2)

    bw = d // 4
    sk = proj[:, off["sk"]:off["sk"] + bw].reshape(b, l, N_HEADS, HEAD_DIM)
    sv = proj[:, off["sv"]:off["sv"] + bw].reshape(b, l, N_HEADS, HEAD_DIM)
    return x2, (sk, sv, ret_new, conv_new, gla_new, ffn_new)


def kernel(x_prompt, x_sample, cache_sb_k, cache_sb_v, page_table, state_ret, state_conv, state_gla,
           state_ffn_conv, c_prompt, c_sample, norm1_g, norm2_g, w_ada, b_ada, w_in, gla_w_lr, gla_b_lr,
           sb_bias, conv_w, conv_b, conv_ln_g, conv_ln_b, w_branch, w_out, w_up, ffn_conv_w, ffn_conv_b,
           w_down, final_g):
    bp, lp, d = x_prompt.shape
    bs, ls, _ = x_sample.shape
    depth = w_in.shape[0]
    d_ff = w_down.shape[1]
    f_pad = -(-d_ff // 512) * 512
    past_len = page_table.shape[1] * PAGE
    off = _layout(d)

    n_c = bp + bs
    m_pad = -(-n_c // 8) * 8
    c_all = jnp.concatenate([c_prompt, c_sample, jnp.zeros((m_pad - n_c, d), F32)], axis=0)
    mod = _ada(c_all, w_ada, b_ada)

    cos_p, sin_p = _rope_tables(jnp.arange(lp))
    cos_s, sin_s = _rope_tables(past_len + jnp.arange(ls))

    xp, xs = x_prompt, x_sample
    out_p = [[] for _ in range(6)]
    out_s = [[] for _ in range(6)]
    for l in range(depth):
        lw = {
            "norm1_g": norm1_g[l], "norm2_g": norm2_g[l],
            "w_in": _prep_w_in(w_in[l], d),
            "gla_w_lr": jnp.pad(gla_w_lr[l], ((0, LANE - GLA_RANK), (0, 0))).astype(BF16),
            "gla_b_lr": gla_b_lr[l].reshape(1, -1),
            "sb_bias": sb_bias[l],
            "conv_w": conv_w[l], "conv_b": conv_b[l], "conv_ln_g": conv_ln_g[l], "conv_ln_b": conv_ln_b[l],
            "w_branch": w_branch[l].astype(BF16), "w_out": w_out[l].astype(BF16),
            "w_up_a": _pad_cols(w_up[l][:, :d_ff], f_pad).astype(BF16),
            "w_up_b": _pad_cols(w_up[l][:, d_ff:], f_pad).astype(BF16),
            "ffn_conv_w": _pad_cols(ffn_conv_w[l], f_pad), "ffn_conv_b": _pad_cols(ffn_conv_b[l], f_pad),
            "w_down": jnp.pad(w_down[l], ((0, f_pad - d_ff), (0, 0))).astype(BF16),
        }
        mod_l = mod[l]
        mods_p = tuple(mod_l[:bp, i * d:(i + 1) * d].reshape(bp, 1, d) for i in range(N_ADA))
        mods_s = tuple(mod_l[bp:n_c, i * d:(i + 1) * d].reshape(bs, 1, d) for i in range(N_ADA))

        xp, st_p = _layer(xp, mods_p, cos_p, sin_p, None, None, None, None, None, lw, off)
        xs, st_s = _layer(xs, mods_s, cos_s, sin_s, state_ret[l], state_conv[l], state_gla[l],
                          _pad_cols(state_ffn_conv[l], f_pad),
                          (cache_sb_k[l], cache_sb_v[l], page_table), lw, off)
        for i in range(6):
            out_p[i].append(st_p[i])
            out_s[i].append(st_s[i])

    y_prompt = _final_norm(xp, final_g)
    y_sample = _final_norm(xs, final_g)
    kp, vp, rp, cp, gp, fp = [jnp.stack(t, axis=0) for t in out_p]
    ks_, vs_, rs_, cs_, gs_, fs_ = [jnp.stack(t, axis=0) for t in out_s]
    fp = fp[..., :d_ff]
    fs_ = fs_[..., :d_ff]
    return (y_prompt, y_sample, kp, vp, rp, cp, gp, fp, ks_, vs_, rs_, cs_, gs_, fs_)
```

```python
import functools

import numpy as np
import jax
import jax.numpy as jnp
from jax import lax
from jax.experimental import pallas as pl
from jax.experimental.pallas import tpu as pltpu

F32 = jnp.float32
BF16 = jnp.bfloat16

EPS = 1e-6
N_HEADS = 4
HEAD_DIM = 128
GLA_DK = 64
GLA_RANK = 16
GLA_TAU = 16.0
GLA_SUB = 16
ROPE_BASE = 10000.0
PAGE = 128
CONV_W = 31
FFN_CONV_W = 3
N_ADA = 6
LANE = 128
VMEM_LIMIT = 56 * 1024 * 1024

C_GATE = 0


def _layout(d):
    bw = d // 4
    off = {}
    o = 4 * d
    for name, w in (("rq", bw), ("rk", bw), ("rv", bw), ("rg", bw),
                    ("sq", bw), ("sk", bw), ("sv", bw),
                    ("ca", bw), ("cb", bw),
                    ("gq", bw // 2), ("gk", bw // 2), ("gv", bw), ("gg", bw),
                    ("glr", 2 * LANE)):
        off[name] = o
        o += w
    off["total"] = o
    return off


def _cparams(n_axes):
    return pltpu.CompilerParams(dimension_semantics=("arbitrary",) * n_axes,
                                vmem_limit_bytes=VMEM_LIMIT)


def _silu(x):
    return x * jax.nn.sigmoid(x)


def _softplus(x):
    return jnp.maximum(x, 0.0) + jnp.log(1.0 + jnp.exp(-jnp.abs(x)))


def _split_bf16(x):
    hi = x.astype(BF16)
    lo = (x - hi.astype(F32)).astype(BF16)
    return hi, lo


def _dot(a, b):
    return jnp.dot(a, b, preferred_element_type=F32)


def _dot_nt(a, b):
    return lax.dot_general(a, b, (((1,), (1,)), ((), ())), preferred_element_type=F32)


def _dot_tn(a, b):
    return lax.dot_general(a, b, (((0,), (0,)), ((), ())), preferred_element_type=F32)


def _ada_body(c_ref, w_ref, b_ref, o_ref):
    s = _silu(c_ref[...]).astype(BF16)
    o_ref[0] = _dot(s, w_ref[0].astype(BF16)) + b_ref[0]


def _ada(c_all, w_ada, b_ada):
    depth, d, n = w_ada.shape
    m = c_all.shape[0]
    tn = 1024
    return pl.pallas_call(
        _ada_body,
        grid=(depth, n // tn),
        in_specs=[pl.BlockSpec((m, d), lambda l, j: (0, 0)),
                  pl.BlockSpec((1, d, tn), lambda l, j: (l, 0, j)),
                  pl.BlockSpec((1, 1, tn), lambda l, j: (l, 0, j))],
        out_specs=pl.BlockSpec((1, m, tn), lambda l, j: (l, 0, j)),
        out_shape=jax.ShapeDtypeStruct((depth, m, n), F32),
        compiler_params=_cparams(2),
        name="ada_mod",
    )(c_all, w_ada, b_ada.reshape(depth, 1, n))


def _modulate(x, g, sh, sc):
    ms = jnp.mean(x * x, axis=-1, keepdims=True)
    y = x * lax.rsqrt(ms + EPS) * g
    return y * (1.0 + sc) + sh


def _modproj_body(x_ref, g_ref, sh_ref, sc_ref, w_ref, o_ref, h_ref):
    @pl.when(pl.program_id(1) == 0)
    def _():
        h = _modulate(x_ref[...], g_ref[...], sh_ref[...], sc_ref[...])
        h_ref[...] = h.reshape(h_ref.shape).astype(BF16)

    o_ref[...] = _dot(h_ref[...], w_ref[...])


def _row_tiling(b, l, tm):
    if b == 1:
        r = min(tm, l)
        return 1, r, l // r, (lambda i: (0, i, 0))
    g = min(max(tm // l, 1), b)
    return g, l, b // g, (lambda i: (i, 0, 0))


def _modproj(x, g, sh, sc, w, layer, tm, tn):
    b, l, d = x.shape
    n = w.shape[2]
    gg, r, nt, xmap = _row_tiling(b, l, tm)
    gmap = (lambda i: (0, 0, 0)) if b == 1 else (lambda i: (i, 0, 0))
    return pl.pallas_call(
        _modproj_body,
        grid=(nt, n // tn),
        in_specs=[pl.BlockSpec((gg, r, d), lambda i, j: xmap(i)),
                  pl.BlockSpec((1, d), lambda i, j: (0, 0)),
                  pl.BlockSpec((gg, 1, d), lambda i, j: gmap(i)),
                  pl.BlockSpec((gg, 1, d), lambda i, j: gmap(i)),
                  pl.BlockSpec((None, d, tn), lambda i, j: (layer, 0, j))],
        out_specs=pl.BlockSpec((gg * r, tn), lambda i, j: (i, j)),
        out_shape=jax.ShapeDtypeStruct((b * l, n), F32),
        scratch_shapes=[pltpu.VMEM((gg * r, d), BF16)],
        compiler_params=_cparams(2),
        name="modproj",
    )(x, g.reshape(1, d), sh, sc, w)


def _ret_body(*refs, c, n, has_s0, cdec):
    if has_s0:
        (rq, rk, rv, rg, cos, sin, dm, qd, kd, s0, o_ref, sn_ref, s_scr, kp, vp) = refs
    else:
        (rq, rk, rv, rg, cos, sin, dm, qd, kd, o_ref, sn_ref, s_scr, kp, vp) = refs
        s0 = None
    i = pl.program_id(1)

    @pl.when(i == 0)
    def _():
        if has_s0:
            s_scr[...] = s0[0]
        else:
            s_scr[...] = jnp.zeros(s_scr.shape, F32)
        if c < LANE:
            kp[...] = jnp.zeros(kp.shape, F32)
            vp[...] = jnp.zeros(vp.shape, F32)

    c2 = cos[...]
    s2 = sin[...]
    for h in range(N_HEADS):
        sl = slice(HEAD_DIM * h, HEAD_DIM * (h + 1))
        q = rq[:, sl]
        k = rk[:, sl]
        v = rv[:, sl]
        gt = rg[:, sl]
        qr = q * c2 + pltpu.roll(q, HEAD_DIM // 2, 1) * s2
        kr = (k * c2 + pltpu.roll(k, HEAD_DIM // 2, 1) * s2) * (HEAD_DIM ** -0.5)
        if c < LANE:
            kp[0, 0:c, sl] = kr
            kp[1, 0:c, sl] = kr * kd[h]
            vp[0:c, sl] = v
            kb = kp[0, :, sl].astype(BF16)
            kdb = kp[1, :, sl].astype(BF16)
            vb = vp[:, sl].astype(BF16)
        else:
            kb = kr.astype(BF16)
            vb = v.astype(BF16)
            kdb = (kr * kd[h]).astype(BF16)
        att = _dot_nt(qr.astype(BF16), kb) * dm[h]
        sh_ = s_scr[h]
        o = _dot(att.astype(BF16), vb) + _dot((qr * qd[h]).astype(BF16), sh_.astype(BF16))
        s_scr[h] = cdec[h] * sh_ + _dot_tn(kdb, vb)
        o = o - jnp.mean(o, axis=-1, keepdims=True)
        o = o * lax.rsqrt(jnp.mean(o * o, axis=-1, keepdims=True) + EPS)
        o_ref[:, sl] = (o * _silu(gt)).astype(o_ref.dtype)

    @pl.when(i == n - 1)
    def _():
        sn_ref[0] = s_scr[...]


def _rope_tables(pos):
    half = HEAD_DIM // 2
    inv = ROPE_BASE ** (-jnp.arange(half, dtype=F32) / half)
    ang = pos.astype(F32)[:, None] * inv[None, :]
    cos = jnp.cos(ang)
    sin = jnp.sin(ang)
    return jnp.concatenate([cos, cos], axis=-1), jnp.concatenate([-sin, sin], axis=-1)


def _ret_consts(c):
    ck = max(c, LANE)
    lg = np.log1p(-np.exp2(-5.0 - np.arange(N_HEADS, dtype=np.float64)))
    idx = np.arange(c, dtype=np.float64)
    rel = idx[:, None] - idx[None, :]
    dmask = np.where(rel[None] >= 0, np.exp(np.maximum(rel, 0.0)[None] * lg[:, None, None]), 0.0)
    dm = np.zeros((N_HEADS, c, ck))
    dm[:, :, :c] = dmask
    qd = np.exp((idx[None, :] + 1.0) * lg[:, None])[:, :, None] * np.ones((1, 1, HEAD_DIM))
    kd = np.exp((c - 1.0 - idx)[None, :] * lg[:, None])[:, :, None] * np.ones((1, 1, HEAD_DIM))
    cdec = tuple(float(v) for v in np.exp(c * lg))
    return (jnp.asarray(dm, F32), jnp.asarray(qd, F32), jnp.asarray(kd, F32), cdec)


def _retention(proj, off, b, l, cos, sin, s0, layer):
    c = 128 if l % 128 == 0 else l
    n = l // c
    ck = max(c, LANE)
    dm, qd, kd, cdec = _ret_consts(c)
    bw = N_HEADS * HEAD_DIM
    cb = lambda name: off[name] // bw

    def pspec(name):
        j = cb(name)
        return pl.BlockSpec((c, bw), lambda bi, i: (bi * n + i, j))

    tab = pl.BlockSpec((c, HEAD_DIM), lambda bi, i: (i, 0))
    const3 = lambda a: pl.BlockSpec(a.shape, lambda bi, i: (0, 0, 0))
    in_specs = [pspec("rq"), pspec("rk"), pspec("rv"), pspec("rg"), tab, tab,
                const3(dm), const3(qd), const3(kd)]
    args = [proj, proj, proj, proj, cos, sin, dm, qd, kd]
    if s0 is not None:
        in_specs.append(pl.BlockSpec((None, 1, N_HEADS, HEAD_DIM, HEAD_DIM),
                                     lambda bi, i: (layer, bi, 0, 0, 0)))
        args.append(s0)
    return pl.pallas_call(
        functools.partial(_ret_body, c=c, n=n, has_s0=s0 is not None, cdec=cdec),
        grid=(b, n),
        in_specs=in_specs,
        out_specs=[pl.BlockSpec((c, bw), lambda bi, i: (bi * n + i, 0)),
                   pl.BlockSpec((1, N_HEADS, HEAD_DIM, HEAD_DIM), lambda bi, i: (bi, 0, 0, 0))],
        out_shape=[jax.ShapeDtypeStruct((b * l, bw), BF16),
                   jax.ShapeDtypeStruct((b, N_HEADS, HEAD_DIM, HEAD_DIM), F32)],
        scratch_shapes=[pltpu.VMEM((N_HEADS, HEAD_DIM, HEAD_DIM), F32),
                        pltpu.VMEM((2, ck, bw), F32), pltpu.VMEM((ck, bw), F32)],
        compiler_params=_cparams(2),
        name="retention",
    )(*args)


def _sb_suffix_matrix(tk):
    j = np.arange(tk)[:, None]
    s = np.arange(tk + LANE)[None, :]
    u = np.where(s < tk, j > s, True)
    return jnp.asarray(u, BF16)


def _sb_block(q, k, v, bias, valid, u, carry):
    tk = k.shape[0]
    z = _dot_nt(q, k) * (HEAD_DIM ** -0.5) + bias
    sp = _softplus(z)
    l1m = -sp
    if valid is not None:
        l1m = jnp.where(valid, l1m, 0.0)
    r = _dot(l1m.astype(BF16), u)
    after = r[:, :tk] + jnp.concatenate([carry] * (tk // LANE), axis=1)
    w = jnp.exp(z - sp + after)
    if valid is not None:
        w = jnp.where(valid, w, 0.0)
    return _dot(w.astype(BF16), v), carry + r[:, tk:]


def _sbp_body(qb_ref, kb_ref, bias_ref, q_ref, k_ref, v_ref, u_ref, o_ref, carry, acc, *, tq, tk, rc):
    h = pl.program_id(0)
    p = pl.program_id(1)
    qb = qb_ref[p]
    kb = kb_ref[p]
    ratio = tq // tk
    first_masked = ratio * qb

    @pl.when(kb == first_masked + ratio - 1)
    def _():
        carry[...] = jnp.zeros(carry.shape, F32)
        acc[...] = jnp.zeros(acc.shape, F32)

    def update(masked):
        kbf = k_ref[...].astype(BF16)
        vbf = v_ref[...].astype(BF16)
        for c in range(tq // rc):
            rs = slice(c * rc, (c + 1) * rc)
            valid = None
            if masked:
                rows = qb * tq + c * rc + lax.broadcasted_iota(jnp.int32, (rc, tk), 0)
                cols = kb * tk + lax.broadcasted_iota(jnp.int32, (rc, tk), 1)
                valid = cols < rows
            pv, cnew = _sb_block(q_ref[rs, :].astype(BF16), kbf, vbf, bias_ref[h], valid,
                                 u_ref[...], carry[rs, :])
            acc[rs, :] += pv
            carry[rs, :] = cnew

    @pl.when(kb >= first_masked)
    def _():
        update(True)

    @pl.when(kb < first_masked)
    def _():
        update(False)

    @pl.when(kb == 0)
    def _():
        o_ref[...] = acc[...].astype(o_ref.dtype)


def _sb_prompt(proj, off, l, bias, tq=512, tk=256, rc=256):
    tq, tk = min(tq, l), min(tk, l)
    rc = min(rc, tq)
    nq = l // tq
    ratio = tq // tk
    qb = np.concatenate([np.full(ratio * (i + 1), i) for i in range(nq)]).astype(np.int32)
    kb = np.concatenate([np.arange(ratio * (i + 1) - 1, -1, -1) for i in range(nq)]).astype(np.int32)
    cq, ck, cv = (off[k] // HEAD_DIM for k in ("sq", "sk", "sv"))
    u = _sb_suffix_matrix(tk)
    grid_spec = pltpu.PrefetchScalarGridSpec(
        num_scalar_prefetch=2,
        grid=(N_HEADS, len(qb)),
        in_specs=[pl.BlockSpec(memory_space=pltpu.SMEM),
                  pl.BlockSpec((tq, HEAD_DIM), lambda h, p, qb, kb: (qb[p], cq + h)),
                  pl.BlockSpec((tk, HEAD_DIM), lambda h, p, qb, kb: (kb[p], ck + h)),
                  pl.BlockSpec((tk, HEAD_DIM), lambda h, p, qb, kb: (kb[p], cv + h)),
                  pl.BlockSpec(u.shape, lambda h, p, qb, kb: (0, 0))],
        out_specs=pl.BlockSpec((tq, HEAD_DIM), lambda h, p, qb, kb: (qb[p], h)),
        scratch_shapes=[pltpu.VMEM((tq, LANE), F32), pltpu.VMEM((tq, HEAD_DIM), F32)])
    return pl.pallas_call(
        functools.partial(_sbp_body, tq=tq, tk=tk, rc=rc),
        grid_spec=grid_spec,
        out_shape=jax.ShapeDtypeStruct((l, N_HEADS * HEAD_DIM), BF16),
        compiler_params=_cparams(2),
        name="sb_prompt",
    )(jnp.asarray(qb), jnp.asarray(kb), bias, proj, proj, proj, u)


def _sbs_suffix_matrix():
    s = np.arange(PAGE + 16)[:, None]
    j = np.arange(PAGE)[None, :]
    return jnp.asarray(np.where(s < PAGE, j > s, True), BF16)


def _sbs_page(k_heads, v_heads, qrows, bias_row, valid, u, carry, l):
    z = None
    for h in range(N_HEADS):
        zh = _dot_nt(k_heads[h], qrows[h])
        z = zh if z is None else z + zh
    z = z * (HEAD_DIM ** -0.5) + bias_row
    sp = _softplus(z)
    l1m = -sp
    if valid is not None:
        l1m = jnp.where(valid, l1m, 0.0)
    r = _dot(u, l1m.astype(BF16))
    w = jnp.exp(z - sp + r[:PAGE] + carry[0:1, :])
    if valid is not None:
        w = jnp.where(valid, w, 0.0)
    wt = jnp.transpose(w)
    pv = [_dot(wt[l * h:l * (h + 1), :].astype(BF16), v_heads[h]) for h in range(N_HEADS)]
    return pv, carry + r[PAGE:PAGE + 8]


def _sbs_body(*refs, l, pps):
    pt_ref, bias_ref, q_ref, kn_ref, vn_ref = refs[:5]
    kpages = refs[5:5 + pps]
    vpages = refs[5 + pps:5 + 2 * pps]
    u_ref, o_ref, carry, acc, qrows, kp, vp = refs[5 + 2 * pps:]
    s = pl.program_id(1)
    n_steps = pl.num_programs(1)
    shift = l.bit_length() - 1
    col = lax.broadcasted_iota(jnp.int32, (1, LANE), 1)
    col_head = lax.shift_right_logical(col, shift)
    bias_row = jnp.zeros((1, LANE), F32)
    for h in range(N_HEADS):
        bias_row = jnp.where(col_head == h, bias_ref[h], bias_row)

    @pl.when(s == 0)
    def _():
        row_head = lax.shift_right_logical(lax.broadcasted_iota(jnp.int32, (LANE, HEAD_DIM), 0), shift)
        for h in range(N_HEADS):
            qt = jnp.concatenate([q_ref[:, HEAD_DIM * h:HEAD_DIM * (h + 1)]] * (LANE // l), axis=0)
            qrows[h] = jnp.where(row_head == h, qt, 0.0).astype(BF16)
        kp[...] = jnp.zeros(kp.shape, F32)
        vp[...] = jnp.zeros(vp.shape, F32)
        kp[0:l, :] = kn_ref[...]
        vp[0:l, :] = vn_ref[...]
        key = lax.broadcasted_iota(jnp.int32, (PAGE, LANE), 0)
        cidx = lax.broadcasted_iota(jnp.int32, (PAGE, LANE), 1)
        valid = key < (cidx & (l - 1))
        heads = lambda ref: [ref[:, HEAD_DIM * h:HEAD_DIM * (h + 1)].astype(BF16) for h in range(N_HEADS)]
        pv, cnew = _sbs_page(heads(kp), heads(vp), [qrows[h] for h in range(N_HEADS)], bias_row, valid,
                             u_ref[...], jnp.zeros(carry.shape, F32), l)
        for h in range(N_HEADS):
            acc[h] = pv[h]
        carry[...] = cnew

    for pp in range(pps):
        heads = lambda ref: [ref[pl.ds(h, PAGE, stride=N_HEADS), :].astype(BF16) for h in range(N_HEADS)]
        pv, cnew = _sbs_page(heads(kpages[pp]), heads(vpages[pp]), [qrows[h] for h in range(N_HEADS)],
                             bias_row, None, u_ref[...], carry[...], l)
        for h in range(N_HEADS):
            acc[h] += pv[h]
        carry[...] = cnew

    @pl.when(s == n_steps - 1)
    def _():
        for h in range(N_HEADS):
            o_ref[:, HEAD_DIM * h:HEAD_DIM * (h + 1)] = acc[h].astype(o_ref.dtype)


def _sb_sample(proj, off, b, l, bias, cache_k, cache_v, layer, page_table, pps=8):
    n_pages = page_table.shape[1]
    pps = min(pps, n_pages)
    assert n_pages % pps == 0 and l & (l - 1) == 0 and N_HEADS * l <= LANE
    bw = N_HEADS * HEAD_DIM
    u = _sbs_suffix_matrix()
    cq, ckn, cvn = (off[k] // bw for k in ("sq", "sk", "sv"))
    pages = lambda c: c.reshape(c.shape[0], c.shape[1], PAGE * N_HEADS, HEAD_DIM)

    def page_spec(pp):
        def imap(bi, s, pt):
            page = n_pages - 1 - (s * pps + pp)
            return (layer, pt[bi * n_pages + page], 0, 0)
        return pl.BlockSpec((None, None, PAGE * N_HEADS, HEAD_DIM), imap)

    in_specs = ([pl.BlockSpec(memory_space=pltpu.SMEM),
                 pl.BlockSpec((l, bw), lambda bi, s, pt: (bi, cq)),
                 pl.BlockSpec((l, bw), lambda bi, s, pt: (bi, ckn)),
                 pl.BlockSpec((l, bw), lambda bi, s, pt: (bi, cvn))]
                + [page_spec(pp) for pp in range(pps)] * 2
                + [pl.BlockSpec(u.shape, lambda bi, s, pt: (0, 0))])
    grid_spec = pltpu.PrefetchScalarGridSpec(
        num_scalar_prefetch=1,
        grid=(b, n_pages // pps),
        in_specs=in_specs,
        out_specs=pl.BlockSpec((l, bw), lambda bi, s, pt: (bi, 0)),
        scratch_shapes=[pltpu.VMEM((8, LANE), F32), pltpu.VMEM((N_HEADS, l, HEAD_DIM), F32),
                        pltpu.VMEM((N_HEADS, LANE, HEAD_DIM), BF16),
                        pltpu.VMEM((PAGE, bw), F32), pltpu.VMEM((PAGE, bw), F32)])
    return pl.pallas_call(
        functools.partial(_sbs_body, l=l, pps=pps),
        grid_spec=grid_spec,
        out_shape=jax.ShapeDtypeStruct((b * l, bw), BF16),
        compiler_params=_cparams(2),
        name="sb_sample",
    )(page_table.reshape(-1), bias, proj, proj, proj, *([pages(cache_k)] * pps), *([pages(cache_v)] * pps), u)


CONV_HALO = 32


def _conv_body(*refs, g, r, n, has_buf):
    if has_buf:
        ca, cb, buf, w_ref, b_ref, lg_ref, lb_ref, o_ref, new_ref, xp = refs
    else:
        ca, cb, w_ref, b_ref, lg_ref, lb_ref, o_ref, new_ref, xp = refs
    i = pl.program_id(1)
    nh = CONV_W - 1
    ch = ca.shape[-1]

    @pl.when(i == 0)
    def _():
        xp[:, 0:CONV_HALO, :] = jnp.zeros((g, CONV_HALO, ch), F32)
        if has_buf:
            xp[:, CONV_HALO - nh:CONV_HALO, :] = buf[...]

    glu = ca[...] * jax.nn.sigmoid(cb[...])
    xp[:, CONV_HALO:CONV_HALO + r, :] = glu.reshape(g, r, ch)
    base = CONV_HALO - nh
    y = jnp.zeros((g, r, ch), F32) + b_ref[...]
    for j in range(CONV_W):
        y = y + w_ref[j:j + 1, :] * xp[:, base + j:base + j + r, :]
    mu = jnp.mean(y, axis=-1, keepdims=True)
    yc = y - mu
    yn = yc * lax.rsqrt(jnp.mean(yc * yc, axis=-1, keepdims=True) + EPS) * lg_ref[...] + lb_ref[...]
    o_ref[...] = _silu(yn).reshape(g * r, ch).astype(o_ref.dtype)
    tail = xp[:, base + r:CONV_HALO + r, :]
    xp[:, base:CONV_HALO, :] = tail

    @pl.when(i == n - 1)
    def _():
        new_ref[...] = tail


def _conv_branch(proj, off, b, l, buf, layer, w, bias, ln_g, ln_b, tm=512):
    ch = w.shape[1]
    g, r, nt, _ = _row_tiling(b, l, tm)
    n = nt if b == 1 else 1
    ja, jb = off["ca"] // ch, off["cb"] // ch
    rowblk = (lambda bi, i: i) if b == 1 else (lambda bi, i: bi)
    grid = (1, nt) if b == 1 else (nt, 1)
    in_specs = [pl.BlockSpec((g * r, ch), lambda bi, i: (rowblk(bi, i), ja)),
                pl.BlockSpec((g * r, ch), lambda bi, i: (rowblk(bi, i), jb))]
    args = [proj, proj]
    if buf is not None:
        in_specs.append(pl.BlockSpec((None, g, CONV_W - 1, ch), lambda bi, i: (layer, bi, 0, 0)))
        args.append(buf)
    const2 = lambda a: pl.BlockSpec(a.shape, lambda bi, i: (0, 0))
    small = [w, bias.reshape(1, ch), ln_g.reshape(1, ch), ln_b.reshape(1, ch)]
    in_specs += [const2(a) for a in small]
    args += small
    return pl.pallas_call(
        functools.partial(_conv_body, g=g, r=r, n=n, has_buf=buf is not None),
        grid=grid,
        in_specs=in_specs,
        out_specs=[pl.BlockSpec((g * r, ch), lambda bi, i: (rowblk(bi, i), 0)),
                   pl.BlockSpec((g, CONV_W - 1, ch), lambda bi, i: (bi, 0, 0))],
        out_shape=[jax.ShapeDtypeStruct((b * l, ch), BF16),
                   jax.ShapeDtypeStruct((b, CONV_W - 1, ch), F32)],
        scratch_shapes=[pltpu.VMEM((g, CONV_HALO + r, ch), F32)],
        compiler_params=_cparams(2),
        name="conv_branch",
    )(*args)


def _gla_consts():
    kd = N_HEADS * GLA_DK
    sel = (np.arange(kd)[:, None] // GLA_DK) == (np.arange(N_HEADS * HEAD_DIM)[None, :] // HEAD_DIM)
    tri = np.arange(GLA_SUB)[:, None] >= np.arange(GLA_SUB)[None, :]
    return jnp.asarray(sel, BF16), jnp.asarray(tri, BF16)


def _gla_sub(q, k, v, gt, glr, wlr, blr, sel, tri, st, pad):
    c = GLA_SUB
    kd = N_HEADS * GLA_DK
    x = _dot(glr.astype(BF16), wlr) + blr
    la = -_softplus(-x) * (1.0 / GLA_TAU)
    row = lax.broadcasted_iota(jnp.int32, (c, kd), 0)
    if pad:
        la = jnp.where(row >= pad, la, 0.0)
    hi, lo = _split_bf16(la)
    bc = _dot(tri, hi) + _dot(tri, lo)
    qs = q * (GLA_DK ** -0.5)
    blocks = []
    for s in range(pad, c):
        e = jnp.where(row >= s, jnp.exp(jnp.minimum(bc - bc[s:s + 1, :], 0.0)), 0.0)
        blocks.append((qs * k[s:s + 1, :] * e).astype(BF16))
    attb = _dot(jnp.concatenate(blocks, axis=0), sel)
    od = jnp.zeros((c, N_HEADS * HEAD_DIM), F32)
    for n_, s in enumerate(range(pad, c)):
        od = od + attb[n_ * c:(n_ + 1) * c, :] * v[s:s + 1, :]
    blast = bc[c - 1:c, :]
    qe = qs * jnp.exp(bc)
    ke = k * jnp.exp(blast - bc)
    lane_head = lax.broadcasted_iota(jnp.int32, (c, kd), 1) // GLA_DK
    stb = st.astype(BF16)
    ds = jnp.zeros(st.shape, F32)
    ois = []
    for h in range(N_HEADS):
        hm = lane_head == h
        ois.append(_dot_nt(jnp.where(hm, qe, 0.0).astype(BF16), stb))
        ds = ds + _dot_tn(v[:, HEAD_DIM * h:HEAD_DIM * (h + 1)].astype(BF16),
                          jnp.where(hm, ke, 0.0).astype(BF16))
    st_new = jnp.exp(blast) * st + ds
    o = od + jnp.concatenate(ois, axis=1)
    outs = []
    for h in range(N_HEADS):
        oh = o[:, HEAD_DIM * h:HEAD_DIM * (h + 1)]
        outs.append(oh * lax.rsqrt(jnp.mean(oh * oh, axis=-1, keepdims=True) + EPS))
    return jnp.concatenate(outs, axis=1) * _silu(gt), st_new


def _gla_body(*refs, rb, n, has_s0):
    if has_s0:
        gq, gk, gv, gg, glr, wlr, blr, sel, tri, s0, o_ref, sn_ref, st = refs
    else:
        gq, gk, gv, gg, glr, wlr, blr, sel, tri, o_ref, sn_ref, st = refs
    i = pl.program_id(1)

    @pl.when(i == 0)
    def _():
        if has_s0:
            st[...] = s0[0]
        else:
            st[...] = jnp.zeros(st.shape, F32)

    c = GLA_SUB
    if rb < c:
        pad = c - rb
        zp = lambda a: jnp.concatenate([jnp.zeros((pad, a.shape[1]), F32), a], axis=0)
        o, st_new = _gla_sub(zp(gq[...]), zp(gk[...]), zp(gv[...]), zp(gg[...]), zp(glr[...]),
                             wlr[...], blr[...], sel[...], tri[...], st[...], pad)
        o_ref[...] = o[pad:, :].astype(o_ref.dtype)
        st[...] = st_new
    else:
        for m in range(rb // c):
            rs = slice(m * c, (m + 1) * c)
            o, st_new = _gla_sub(gq[rs, :], gk[rs, :], gv[rs, :], gg[rs, :], glr[rs, :],
                                 wlr[...], blr[...], sel[...], tri[...], st[...], 0)
            o_ref[rs, :] = o.astype(o_ref.dtype)
            st[...] = st_new

    @pl.when(i == n - 1)
    def _():
        sn_ref[0] = st[...]


def _gla(proj, off, b, l, wlr, blr, s0t, layer, rb=128):
    rb = min(rb, l)
    n = l // rb
    kd = N_HEADS * GLA_DK
    bw = N_HEADS * HEAD_DIM
    sel, tri = _gla_consts()

    def pspec(name, width):
        j = off[name] // width
        return pl.BlockSpec((rb, width), lambda bi, i: (bi * n + i, j))

    const2 = lambda a: pl.BlockSpec(a.shape, lambda bi, i: (0, 0))
    in_specs = [pspec("gq", kd), pspec("gk", kd), pspec("gv", bw), pspec("gg", bw), pspec("glr", LANE),
                const2(wlr), const2(blr), const2(sel), const2(tri)]
    args = [proj, proj, proj, proj, proj, wlr, blr, sel, tri]
    if s0t is not None:
        in_specs.append(pl.BlockSpec((None, 1, HEAD_DIM, kd), lambda bi, i: (layer, bi, 0, 0)))
        args.append(s0t)
    return pl.pallas_call(
        functools.partial(_gla_body, rb=rb, n=n, has_s0=s0t is not None),
        grid=(b, n),
        in_specs=in_specs,
        out_specs=[pl.BlockSpec((rb, bw), lambda bi, i: (bi * n + i, 0)),
                   pl.BlockSpec((1, HEAD_DIM, kd), lambda bi, i: (bi, 0, 0))],
        out_shape=[jax.ShapeDtypeStruct((b * l, bw), BF16),
                   jax.ShapeDtypeStruct((b, HEAD_DIM, kd), F32)],
        scratch_shapes=[pltpu.VMEM((HEAD_DIM, kd), F32)],
        compiler_params=_cparams(2),
        name="gla",
    )(*args)


def _merge_body(a_ref, b_ref, c_ref, d_ref, g0, g1, g2, g3, w_ref, o_ref):
    acc = None
    bw = a_ref.shape[1]
    for i, (br, gl) in enumerate(((a_ref, g0), (b_ref, g1), (c_ref, g2), (d_ref, g3))):
        t = jax.nn.sigmoid(gl[...]) * _dot(br[...], w_ref[bw * i:bw * (i + 1), :])
        acc = t if acc is None else acc + t
    o_ref[...] = acc.astype(o_ref.dtype)


def _merge(branches, proj, wb, layer, tm=512, tn=1024):
    rows, bw = branches[0].shape
    d = wb.shape[2]
    tm = min(tm, rows)
    nj = d // tn
    br_spec = pl.BlockSpec((tm, bw), lambda i, j: (i, 0))
    gate_specs = [pl.BlockSpec((tm, tn), (lambda i, j, q=q: (i, q * nj + j))) for q in range(4)]
    return pl.pallas_call(
        _merge_body,
        grid=(rows // tm, nj),
        in_specs=[br_spec] * 4 + gate_specs + [pl.BlockSpec((None, 4 * bw, tn), lambda i, j: (layer, 0, j))],
        out_specs=pl.BlockSpec((tm, tn), lambda i, j: (i, j)),
        out_shape=jax.ShapeDtypeStruct((rows, d), BF16),
        compiler_params=_cparams(2),
        name="merge",
    )(*branches, proj, proj, proj, proj, wb)


def _resid_body(a_ref, w_ref, x_ref, g_ref, o_ref):
    y = _dot(a_ref[...], w_ref[...])
    o_ref[...] = x_ref[...] + g_ref[...] * y.reshape(o_ref.shape)


def _resid_proj(a, w, layer, x, gate, tm=1024, tn=512):
    b, l, d = x.shape
    k = a.shape[1]
    g, r, nt, xmap = _row_tiling(b, l, tm)
    gmap = (lambda i: (0, 0)) if b == 1 else (lambda i: (i, 0))
    return pl.pallas_call(
        _resid_body,
        grid=(nt, d // tn),
        in_specs=[pl.BlockSpec((g * r, k), lambda i, j: (i, 0)),
                  pl.BlockSpec((None, k, tn), lambda i, j: (layer, 0, j)),
                  pl.BlockSpec((g, r, tn), lambda i, j: xmap(i)[:2] + (j,)),
                  pl.BlockSpec((g, 1, tn), lambda i, j: gmap(i) + (j,))],
        out_specs=pl.BlockSpec((g, r, tn), lambda i, j: xmap(i)[:2] + (j,)),
        out_shape=jax.ShapeDtypeStruct((b, l, d), F32),
        compiler_params=_cparams(2),
        name="resid_proj",
    )(a, w, x, gate)


FFN_HALO = 8


def _ffn_body(*refs, g, r, has_buf):
    if has_buf:
        (x_ref, n_ref, sh_ref, sc_ref, wa_ref, wb_ref, cw_ref, cb_ref, buf_ref,
         y_ref, new_ref, h_ref, ap, carry) = refs
    else:
        (x_ref, n_ref, sh_ref, sc_ref, wa_ref, wb_ref, cw_ref, cb_ref,
         y_ref, new_ref, h_ref, ap, carry) = refs
    i = pl.program_id(0)
    j = pl.program_id(1)
    nh = FFN_CONV_W - 1
    tn = wa_ref.shape[1]

    @pl.when(j == 0)
    def _():
        h = _modulate(x_ref[...], n_ref[...], sh_ref[...], sc_ref[...])
        h_ref[...] = h.reshape(h_ref.shape).astype(BF16)

    hb = h_ref[...]
    a3 = _dot(hb, wa_ref[...]).reshape(g, r, tn)
    b3 = _dot(hb, wb_ref[...]).reshape(g, r, tn)
    if has_buf:
        halo = buf_ref[...]
    else:
        halo = jnp.where(i == 0, 0.0, carry[j])
    ap[:, FFN_HALO - nh:FFN_HALO, :] = halo
    ap[:, FFN_HALO:FFN_HALO + r, :] = a3
    conv = (cb_ref[...] + cw_ref[0:1, :] * ap[:, FFN_HALO - 2:FFN_HALO - 2 + r, :]
            + cw_ref[1:2, :] * ap[:, FFN_HALO - 1:FFN_HALO - 1 + r, :] + cw_ref[2:3, :] * a3)
    y_ref[...] = (_silu(conv) * b3).reshape(g * r, tn).astype(y_ref.dtype)
    tail = a3[:, r - nh:r, :]
    new_ref[...] = tail
    if not has_buf:
        carry[j] = tail


def _ffn_up(x, ng, sh, sc, w_up, layer, cw, cb, buf, tm=1024, tn=512):
    b, l, d = x.shape
    f = w_up.shape[2] // 2
    g, r, nt, xmap = _row_tiling(b, l, tm)
    gmap = (lambda i: (0, 0, 0)) if b == 1 else (lambda i: (i, 0, 0))
    nh = FFN_CONV_W - 1
    nj = f // tn
    in_specs = [pl.BlockSpec((g, r, d), lambda i, j: xmap(i)),
                pl.BlockSpec((1, d), lambda i, j: (0, 0)),
                pl.BlockSpec((g, 1, d), lambda i, j: gmap(i)),
                pl.BlockSpec((g, 1, d), lambda i, j: gmap(i)),
                pl.BlockSpec((None, d, tn), lambda i, j: (layer, 0, j)),
                pl.BlockSpec((None, d, tn), lambda i, j: (layer, 0, nj + j)),
                pl.BlockSpec((FFN_CONV_W, tn), lambda i, j: (0, j)),
                pl.BlockSpec((1, tn), lambda i, j: (0, j))]
    args = [x, ng.reshape(1, d), sh, sc, w_up, w_up, cw, cb.reshape(1, f)]
    if buf is not None:
        in_specs.append(pl.BlockSpec((None, g, nh, tn), lambda i, j: (layer, i, 0, j)))
        args.append(buf)
    n_slots = nt if b == 1 else 1
    newmap = (lambda i, j: (i, 0, 0, j)) if b == 1 else (lambda i, j: (0, i, 0, j))
    return pl.pallas_call(
        functools.partial(_ffn_body, g=g, r=r, has_buf=buf is not None),
        grid=(nt, nj),
        in_specs=in_specs,
        out_specs=[pl.BlockSpec((g * r, tn), lambda i, j: (i, j)),
                   pl.BlockSpec((None, g, nh, tn), newmap)],
        out_shape=[jax.ShapeDtypeStruct((b * l, f), BF16),
                   jax.ShapeDtypeStruct((n_slots, b, nh, f), F32)],
        scratch_shapes=[pltpu.VMEM((g * r, d), BF16),
                        pltpu.VMEM((g, FFN_HALO + r, tn), F32),
                        pltpu.VMEM((nj, g, nh, tn), F32)],
        compiler_params=_cparams(2),
        name="ffn_up",
    )(*args)


def _final_body(x_ref, g_ref, o_ref):
    x = x_ref[...]
    o_ref[...] = x * lax.rsqrt(jnp.mean(x * x, axis=-1, keepdims=True) + EPS) * g_ref[...]


def _final_norm(x, g, tm=512):
    b, l, d = x.shape
    gg, r, nt, xmap = _row_tiling(b, l, tm)
    return pl.pallas_call(
        _final_body,
        grid=(nt,),
        in_specs=[pl.BlockSpec((gg, r, d), xmap), pl.BlockSpec((1, d), lambda i: (0, 0))],
        out_specs=pl.BlockSpec((gg, r, d), xmap),
        out_shape=jax.ShapeDtypeStruct((b, l, d), F32),
        compiler_params=_cparams(1),
        name="final_norm",
    )(x, g.reshape(1, d))


def _cast_body(x_ref, o_ref):
    o_ref[...] = x_ref[...].astype(o_ref.dtype)


def _cast_stack(w, tr=512):
    depth, r, c = w.shape
    spec = pl.BlockSpec((1, tr, c), lambda l, i: (l, i, 0))
    return pl.pallas_call(
        _cast_body, grid=(depth, r // tr), in_specs=[spec], out_specs=spec,
        out_shape=jax.ShapeDtypeStruct(w.shape, BF16), compiler_params=_cparams(2), name="cast_w",
    )(w)


def _prep_up_body(x_ref, o_ref):
    f = x_ref.shape[2]
    o_ref[0, :, :f] = x_ref[0].astype(BF16)
    o_ref[0, :, f:] = jnp.zeros((o_ref.shape[1], o_ref.shape[2] - f), BF16)


def _prep_w_up(w_up, f_pad, tr=256):
    depth, d, n = w_up.shape
    d_ff = n // 2
    return pl.pallas_call(
        _prep_up_body, grid=(depth, d // tr, 2),
        in_specs=[pl.BlockSpec((1, tr, d_ff), lambda l, i, h: (l, i, h))],
        out_specs=pl.BlockSpec((1, tr, f_pad), lambda l, i, h: (l, i, h)),
        out_shape=jax.ShapeDtypeStruct((depth, d, 2 * f_pad), BF16),
        compiler_params=_cparams(3), name="prep_w_up",
    )(w_up)


def _prep_down_body(x_ref, o_ref):
    f = x_ref.shape[1]
    o_ref[0, :f, :] = x_ref[0].astype(BF16)
    o_ref[0, f:, :] = jnp.zeros((o_ref.shape[1] - f, o_ref.shape[2]), BF16)


def _prep_w_down(w_down, f_pad, tc=256):
    depth, d_ff, d = w_down.shape
    return pl.pallas_call(
        _prep_down_body, grid=(depth, d // tc),
        in_specs=[pl.BlockSpec((1, d_ff, tc), lambda l, j: (l, 0, j))],
        out_specs=pl.BlockSpec((1, f_pad, tc), lambda l, j: (l, 0, j)),
        out_shape=jax.ShapeDtypeStruct((depth, f_pad, d), BF16),
        compiler_params=_cparams(2), name="prep_w_down",
    )(w_down)


PREP_TN = 1024


def _prep_in_body(main_ref, extra_ref, o_ref, *, n_gate, n_core):
    j = pl.program_id(2)

    @pl.when(j < n_gate)
    def _():
        cat = jnp.concatenate([main_ref[0], extra_ref[0]], axis=1)
        o_ref[0] = cat[:, GLA_RANK:GLA_RANK + PREP_TN].astype(BF16)

    @pl.when((j >= n_gate) & (j < n_gate + n_core))
    def _():
        o_ref[0] = main_ref[0].astype(BF16)

    @pl.when(j == n_gate + n_core)
    def _():
        lane = lax.broadcasted_iota(jnp.int32, main_ref.shape[1:], 1)
        o_ref[0] = jnp.where(lane < GLA_RANK, main_ref[0], 0.0).astype(BF16)


def _prep_w_in(w_in, off, tr=512):
    depth, d, n_in = w_in.shape
    core = off["glr"] - off["rq"]
    assert core % PREP_TN == 0 and (4 * d) % PREP_TN == 0 and n_in == core + GLA_RANK + 4 * d
    n_gate, n_core = 4 * d // PREP_TN, core // PREP_TN
    c0 = core // PREP_TN

    def main_map(l, i, j):
        return (l, i, jnp.where(j < n_gate, c0 + j, jnp.where(j < n_gate + n_core, j - n_gate, c0)))

    def extra_map(l, i, j):
        return (l, i, jnp.where(j < n_gate, (c0 + j + 1) * (PREP_TN // LANE), 0))

    return pl.pallas_call(
        functools.partial(_prep_in_body, n_gate=n_gate, n_core=n_core),
        grid=(depth, d // tr, n_gate + n_core + 1),
        in_specs=[pl.BlockSpec((1, tr, PREP_TN), main_map), pl.BlockSpec((1, tr, LANE), extra_map)],
        out_specs=pl.BlockSpec((1, tr, PREP_TN), lambda l, i, j: (l, i, j)),
        out_shape=jax.ShapeDtypeStruct((depth, d, off["total"]), BF16),
        compiler_params=_cparams(3), name="prep_w_in",
    )(w_in, w_in)


def _pad_cols(a, n):
    return jnp.pad(a, [(0, 0)] * (a.ndim - 1) + [(0, n - a.shape[-1])])


def _layer(x, mods, cos, sin, layer, states, sb_cache, sw, lw, off):
    b, l, d = x.shape
    sh1, sc1, g1, sh2, sc2, g2 = mods
    ret_s0, conv_buf, gla_s0t, ffn_buf = states if states is not None else (None,) * 4
    proj = _modproj(x, lw["norm1_g"], sh1, sc1, sw["w_in"], layer, tm=1024, tn=768)

    o_a, ret_new = _retention(proj, off, b, l, cos, sin, ret_s0, layer)
    if sb_cache is None:
        o_b = _sb_prompt(proj, off, l, lw["sb_bias"])
    else:
        cache_k, cache_v, page_table = sb_cache
        o_b = _sb_sample(proj, off, b, l, lw["sb_bias"], cache_k, cache_v, layer, page_table)
    o_c, conv_new = _conv_branch(proj, off, b, l, conv_buf, layer, lw["conv_w"], lw["conv_b"],
                                 lw["conv_ln_g"], lw["conv_ln_b"])
    o_d, gla_t = _gla(proj, off, b, l, lw["gla_w_lr"], lw["gla_b_lr"], gla_s0t, layer)
    gla_new = gla_t.reshape(b, HEAD_DIM, N_HEADS, GLA_DK).transpose(0, 2, 3, 1)

    merged = _merge((o_a, o_b, o_c, o_d), proj, sw["w_branch"], layer)
    x1 = _resid_proj(merged, sw["w_out"], layer, x, g1)

    y, ffn_slots = _ffn_up(x1, lw["norm2_g"], sh2, sc2, sw["w_up"], layer,
                           lw["ffn_conv_w"], lw["ffn_conv_b"], ffn_buf)
    x2 = _resid_proj(y, sw["w_down"], layer, x1, g2)

    bw = d // 4
    sk = proj[:, off["sk"]:off["sk"] + bw].reshape(b, l, N_HEADS, HEAD_DIM)
    sv = proj[:, off["sv"]:off["sv"] + bw].reshape(b, l, N_HEADS, HEAD_DIM)
    return x2, (sk, sv, ret_new, conv_new, gla_new, ffn_slots[-1])


def kernel(x_prompt, x_sample, cache_sb_k, cache_sb_v, page_table, state_ret, state_conv, state_gla,
           state_ffn_conv, c_prompt, c_sample, norm1_g, norm2_g, w_ada, b_ada, w_in, gla_w_lr, gla_b_lr,
           sb_bias, conv_w, conv_b, conv_ln_g, conv_ln_b, w_branch, w_out, w_up, ffn_conv_w, ffn_conv_b,
           w_down, final_g):
    bp, lp, d = x_prompt.shape
    bs, ls, _ = x_sample.shape
    depth = w_in.shape[0]
    d_ff = w_down.shape[1]
    f_pad = -(-d_ff // 512) * 512
    past_len = page_table.shape[1] * PAGE
    off = _layout(d)

    n_c = bp + bs
    m_pad = -(-n_c // 8) * 8
    c_all = jnp.concatenate([c_prompt, c_sample, jnp.zeros((m_pad - n_c, d), F32)], axis=0)
    mod = _ada(c_all, w_ada, b_ada)

    cos_p, sin_p = _rope_tables(jnp.arange(lp))
    cos_s, sin_s = _rope_tables(past_len + jnp.arange(ls))

    sw = {
        "w_in": _prep_w_in(w_in, off),
        "w_branch": _cast_stack(w_branch.reshape(depth, -1, d)),
        "w_out": _cast_stack(w_out),
        "w_up": _prep_w_up(w_up, f_pad),
        "w_down": _prep_w_down(w_down, f_pad),
    }
    gla_t = state_gla.transpose(0, 1, 4, 2, 3).reshape(depth, bs, HEAD_DIM, N_HEADS * GLA_DK)
    states_s = (state_ret, state_conv, gla_t, _pad_cols(state_ffn_conv, f_pad))
    sb_cache = (cache_sb_k, cache_sb_v, page_table)

    xp, xs = x_prompt, x_sample
    out_p = [[] for _ in range(6)]
    out_s = [[] for _ in range(6)]
    for l in range(depth):
        lw = {
            "norm1_g": norm1_g[l], "norm2_g": norm2_g[l],
            "gla_w_lr": jnp.pad(gla_w_lr[l], ((0, LANE - GLA_RANK), (0, 0))).astype(BF16),
            "gla_b_lr": gla_b_lr[l].reshape(1, -1),
            "sb_bias": sb_bias[l],
            "conv_w": conv_w[l], "conv_b": conv_b[l], "conv_ln_g": conv_ln_g[l], "conv_ln_b": conv_ln_b[l],
            "ffn_conv_w": _pad_cols(ffn_conv_w[l], f_pad), "ffn_conv_b": _pad_cols(ffn_conv_b[l], f_pad),
        }
        mod_l = mod[l]
        mods_p = tuple(mod_l[:bp, i * d:(i + 1) * d].reshape(bp, 1, d) for i in range(N_ADA))
        mods_s = tuple(mod_l[bp:n_c, i * d:(i + 1) * d].reshape(bs, 1, d) for i in range(N_ADA))

        xp, st_p = _layer(xp, mods_p, cos_p, sin_p, l, None, None, sw, lw, off)
        xs, st_s = _layer(xs, mods_s, cos_s, sin_s, l, states_s, sb_cache, sw, lw, off)
        for i in range(6):
            out_p[i].append(st_p[i])
            out_s[i].append(st_s[i])

    y_prompt = _final_norm(xp, final_g)
    y_sample = _final_norm(xs, final_g)
    kp, vp, rp, cp, gp, fp = [jnp.stack(t, axis=0) for t in out_p]
    ks_, vs_, rs_, cs_, gs_, fs_ = [jnp.stack(t, axis=0) for t in out_s]
    fp = fp[..., :d_ff]
    fs_ = fs_[..., :d_ff]
    return (y_prompt, y_sample, kp, vp, rp, cp, gp, fp, ks_, vs_, rs_, cs_, gs_, fs_)
```

```python
import functools

import numpy as np
import jax
import jax.numpy as jnp
from jax import lax
from jax.experimental import pallas as pl
from jax.experimental.pallas import tpu as pltpu

F32 = jnp.float32
BF16 = jnp.bfloat16

EPS = 1e-6
N_HEADS = 4
HEAD_DIM = 128
GLA_DK = 64
GLA_RANK = 16
GLA_TAU = 16.0
GLA_SUB = 16
ROPE_BASE = 10000.0
PAGE = 128
CONV_W = 31
FFN_CONV_W = 3
N_ADA = 6
LANE = 128
VMEM_LIMIT = 56 * 1024 * 1024

C_GATE = 0


def _layout(d):
    bw = d // 4
    off = {}
    o = 4 * d
    for name, w in (("rq", bw), ("rk", bw), ("rv", bw), ("rg", bw),
                    ("sq", bw), ("sk", bw), ("sv", bw),
                    ("ca", bw), ("cb", bw),
                    ("gq", bw // 2), ("gk", bw // 2), ("gv", bw), ("gg", bw),
                    ("glr", 2 * LANE)):
        off[name] = o
        o += w
    off["total"] = o
    return off


def _cparams(n_axes):
    return pltpu.CompilerParams(dimension_semantics=("arbitrary",) * n_axes,
                                vmem_limit_bytes=VMEM_LIMIT)


def _silu(x):
    return x * jax.nn.sigmoid(x)


LOG2E = 1.4426950408889634


def _softplus(x):
    return jnp.maximum(x, 0.0) + jnp.log(1.0 + jnp.exp2(jnp.abs(x) * (-LOG2E)))


def _split_bf16(x):
    hi = x.astype(BF16)
    lo = (x - hi.astype(F32)).astype(BF16)
    return hi, lo


def _dot(a, b):
    return jnp.dot(a, b, preferred_element_type=F32)


def _dot_nt(a, b):
    return lax.dot_general(a, b, (((1,), (1,)), ((), ())), preferred_element_type=F32)


def _dot_tn(a, b):
    return lax.dot_general(a, b, (((0,), (0,)), ((), ())), preferred_element_type=F32)


def _ada_body(c_ref, w_ref, b_ref, o_ref):
    s = _silu(c_ref[...]).astype(BF16)
    o_ref[0] = _dot(s, w_ref[0].astype(BF16)) + b_ref[0]


def _ada(c_all, w_ada, b_ada):
    depth, d, n = w_ada.shape
    m = c_all.shape[0]
    tn = 1024
    return pl.pallas_call(
        _ada_body,
        grid=(depth, n // tn),
        in_specs=[pl.BlockSpec((m, d), lambda l, j: (0, 0)),
                  pl.BlockSpec((1, d, tn), lambda l, j: (l, 0, j)),
                  pl.BlockSpec((1, 1, tn), lambda l, j: (l, 0, j))],
        out_specs=pl.BlockSpec((1, m, tn), lambda l, j: (l, 0, j)),
        out_shape=jax.ShapeDtypeStruct((depth, m, n), F32),
        compiler_params=_cparams(2),
        name="ada_mod",
    )(c_all, w_ada, b_ada.reshape(depth, 1, n))


def _modulate(x, g, sh, sc):
    ms = jnp.mean(x * x, axis=-1, keepdims=True)
    y = x * lax.rsqrt(ms + EPS) * g
    return y * (1.0 + sc) + sh


def _modproj_body(x_ref, g_ref, sh_ref, sc_ref, w_ref, o_ref, h_ref):
    @pl.when(pl.program_id(1) == 0)
    def _():
        h = _modulate(x_ref[...], g_ref[...], sh_ref[...], sc_ref[...])
        h_ref[...] = h.reshape(h_ref.shape).astype(BF16)

    o_ref[...] = _dot_nt(h_ref[...], w_ref[...])


def _row_tiling(b, l, tm):
    if b == 1:
        r = min(tm, l)
        return 1, r, l // r, (lambda i: (0, i, 0))
    g = min(max(tm // l, 1), b)
    return g, l, b // g, (lambda i: (i, 0, 0))


def _modproj(x, g, sh, sc, w, layer, tm, tn):
    b, l, d = x.shape
    n = w.shape[1]
    gg, r, nt, xmap = _row_tiling(b, l, tm)
    gmap = (lambda i: (0, 0, 0)) if b == 1 else (lambda i: (i, 0, 0))
    return pl.pallas_call(
        _modproj_body,
        grid=(nt, n // tn),
        in_specs=[pl.BlockSpec((gg, r, d), lambda i, j: xmap(i)),
                  pl.BlockSpec((1, d), lambda i, j: (0, 0)),
                  pl.BlockSpec((gg, 1, d), lambda i, j: gmap(i)),
                  pl.BlockSpec((gg, 1, d), lambda i, j: gmap(i)),
                  pl.BlockSpec((None, tn, d), lambda i, j: (layer, j, 0))],
        out_specs=pl.BlockSpec((gg * r, tn), lambda i, j: (i, j)),
        out_shape=jax.ShapeDtypeStruct((b * l, n), F32),
        scratch_shapes=[pltpu.VMEM((gg * r, d), BF16)],
        compiler_params=_cparams(2),
        name="modproj",
    )(x, g.reshape(1, d), sh, sc, w)


def _ret_body(*refs, c, n, has_s0, cdec):
    if has_s0:
        (rq, rk, rv, rg, cos, sin, dm, qd, kd, s0, o_ref, sn_ref, s_scr, kp, vp) = refs
    else:
        (rq, rk, rv, rg, cos, sin, dm, qd, kd, o_ref, sn_ref, s_scr, kp, vp) = refs
        s0 = None
    i = pl.program_id(1)

    @pl.when(i == 0)
    def _():
        if has_s0:
            s_scr[...] = s0[0]
        else:
            s_scr[...] = jnp.zeros(s_scr.shape, F32)
        if c < LANE:
            kp[...] = jnp.zeros(kp.shape, F32)
            vp[...] = jnp.zeros(vp.shape, F32)

    c2 = cos[...]
    s2 = sin[...]
    for h in range(N_HEADS):
        sl = slice(HEAD_DIM * h, HEAD_DIM * (h + 1))
        q = rq[:, sl]
        k = rk[:, sl]
        v = rv[:, sl]
        gt = rg[:, sl]
        qr = q * c2 + pltpu.roll(q, HEAD_DIM // 2, 1) * s2
        kr = (k * c2 + pltpu.roll(k, HEAD_DIM // 2, 1) * s2) * (HEAD_DIM ** -0.5)
        if c < LANE:
            kp[0, 0:c, sl] = kr
            kp[1, 0:c, sl] = kr * kd[h]
            vp[0:c, sl] = v
            kb = kp[0, :, sl].astype(BF16)
            kdb = kp[1, :, sl].astype(BF16)
            vb = vp[:, sl].astype(BF16)
        else:
            kb = kr.astype(BF16)
            vb = v.astype(BF16)
            kdb = (kr * kd[h]).astype(BF16)
        att = _dot_nt(qr.astype(BF16), kb) * dm[h]
        sh_ = s_scr[h]
        o = _dot(att.astype(BF16), vb) + _dot((qr * qd[h]).astype(BF16), sh_.astype(BF16))
        s_scr[h] = cdec[h] * sh_ + _dot_tn(kdb, vb)
        o = o - jnp.mean(o, axis=-1, keepdims=True)
        o = o * lax.rsqrt(jnp.mean(o * o, axis=-1, keepdims=True) + EPS)
        o_ref[:, sl] = (o * _silu(gt)).astype(o_ref.dtype)

    @pl.when(i == n - 1)
    def _():
        sn_ref[0] = s_scr[...]


def _rope_tables(pos):
    half = HEAD_DIM // 2
    inv = ROPE_BASE ** (-jnp.arange(half, dtype=F32) / half)
    ang = pos.astype(F32)[:, None] * inv[None, :]
    cos = jnp.cos(ang)
    sin = jnp.sin(ang)
    return jnp.concatenate([cos, cos], axis=-1), jnp.concatenate([-sin, sin], axis=-1)


def _ret_consts(c):
    ck = max(c, LANE)
    lg = np.log1p(-np.exp2(-5.0 - np.arange(N_HEADS, dtype=np.float64)))
    idx = np.arange(c, dtype=np.float64)
    rel = idx[:, None] - idx[None, :]
    dmask = np.where(rel[None] >= 0, np.exp(np.maximum(rel, 0.0)[None] * lg[:, None, None]), 0.0)
    dm = np.zeros((N_HEADS, c, ck))
    dm[:, :, :c] = dmask
    qd = np.exp((idx[None, :] + 1.0) * lg[:, None])[:, :, None] * np.ones((1, 1, HEAD_DIM))
    kd = np.exp((c - 1.0 - idx)[None, :] * lg[:, None])[:, :, None] * np.ones((1, 1, HEAD_DIM))
    cdec = tuple(float(v) for v in np.exp(c * lg))
    return (jnp.asarray(dm, F32), jnp.asarray(qd, F32), jnp.asarray(kd, F32), cdec)


def _retention(proj, off, b, l, cos, sin, s0, layer):
    c = 128 if l % 128 == 0 else l
    n = l // c
    ck = max(c, LANE)
    dm, qd, kd, cdec = _ret_consts(c)
    bw = N_HEADS * HEAD_DIM
    cb = lambda name: off[name] // bw

    def pspec(name):
        j = cb(name)
        return pl.BlockSpec((c, bw), lambda bi, i: (bi * n + i, j))

    tab = pl.BlockSpec((c, HEAD_DIM), lambda bi, i: (i, 0))
    const3 = lambda a: pl.BlockSpec(a.shape, lambda bi, i: (0, 0, 0))
    in_specs = [pspec("rq"), pspec("rk"), pspec("rv"), pspec("rg"), tab, tab,
                const3(dm), const3(qd), const3(kd)]
    args = [proj, proj, proj, proj, cos, sin, dm, qd, kd]
    if s0 is not None:
        in_specs.append(pl.BlockSpec((None, 1, N_HEADS, HEAD_DIM, HEAD_DIM),
                                     lambda bi, i: (layer, bi, 0, 0, 0)))
        args.append(s0)
    return pl.pallas_call(
        functools.partial(_ret_body, c=c, n=n, has_s0=s0 is not None, cdec=cdec),
        grid=(b, n),
        in_specs=in_specs,
        out_specs=[pl.BlockSpec((c, bw), lambda bi, i: (bi * n + i, 0)),
                   pl.BlockSpec((1, N_HEADS, HEAD_DIM, HEAD_DIM), lambda bi, i: (bi, 0, 0, 0))],
        out_shape=[jax.ShapeDtypeStruct((b * l, bw), BF16),
                   jax.ShapeDtypeStruct((b, N_HEADS, HEAD_DIM, HEAD_DIM), F32)],
        scratch_shapes=[pltpu.VMEM((N_HEADS, HEAD_DIM, HEAD_DIM), F32),
                        pltpu.VMEM((2, ck, bw), F32), pltpu.VMEM((ck, bw), F32)],
        compiler_params=_cparams(2),
        name="retention",
    )(*args)


def _sb_suffix_matrix(tk):
    j = np.arange(tk)[:, None]
    s = np.arange(tk + LANE)[None, :]
    u = np.where(s < tk, j > s, True)
    return jnp.asarray(u, BF16)


def _sb_block(q, k, v, bias, valid, u, carry):
    tk = k.shape[0]
    z = _dot_nt(q, k) + bias
    sp = _softplus(z)
    spm = sp if valid is None else jnp.where(valid, sp, 0.0)
    r = _dot(spm.astype(BF16), u)
    newer = r[:, :tk] + jnp.concatenate([carry] * (tk // LANE), axis=1)
    w = jnp.exp((z - sp) - newer)
    if valid is not None:
        w = jnp.where(valid, w, 0.0)
    return _dot(w.astype(BF16), v), carry + r[:, tk:]


def _sbp_body(qb_ref, kb_ref, bias_ref, q_ref, k_ref, v_ref, u_ref, o_ref, carry, acc, *, tq, tk, rc):
    h = pl.program_id(0)
    p = pl.program_id(1)
    qb = qb_ref[p]
    kb = kb_ref[p]
    ratio = tq // tk
    first_masked = ratio * qb

    @pl.when(kb == first_masked + ratio - 1)
    def _():
        carry[...] = jnp.zeros(carry.shape, F32)
        acc[...] = jnp.zeros(acc.shape, F32)

    def update(masked):
        kbf = k_ref[...].astype(BF16)
        vbf = v_ref[...].astype(BF16)
        for c in range(tq // rc):
            rs = slice(c * rc, (c + 1) * rc)
            valid = None
            if masked:
                rows = qb * tq + c * rc + lax.broadcasted_iota(jnp.int32, (rc, tk), 0)
                cols = kb * tk + lax.broadcasted_iota(jnp.int32, (rc, tk), 1)
                valid = cols < rows
            qs = (q_ref[rs, :] * (HEAD_DIM ** -0.5)).astype(BF16)
            pv, cnew = _sb_block(qs, kbf, vbf, bias_ref[h], valid, u_ref[...], carry[rs, :])
            acc[rs, :] += pv
            carry[rs, :] = cnew

    @pl.when(kb >= first_masked)
    def _():
        update(True)

    @pl.when(kb < first_masked)
    def _():
        update(False)

    @pl.when(kb == 0)
    def _():
        o_ref[...] = acc[...].astype(o_ref.dtype)


def _sb_prompt(proj, off, l, bias, tq=512, tk=512, rc=256):
    tq, tk = min(tq, l), min(tk, l)
    rc = min(rc, tq)
    nq = l // tq
    ratio = tq // tk
    qb = np.concatenate([np.full(ratio * (i + 1), i) for i in range(nq)]).astype(np.int32)
    kb = np.concatenate([np.arange(ratio * (i + 1) - 1, -1, -1) for i in range(nq)]).astype(np.int32)
    cq, ck, cv = (off[k] // HEAD_DIM for k in ("sq", "sk", "sv"))
    u = _sb_suffix_matrix(tk)
    grid_spec = pltpu.PrefetchScalarGridSpec(
        num_scalar_prefetch=2,
        grid=(N_HEADS, len(qb)),
        in_specs=[pl.BlockSpec(memory_space=pltpu.SMEM),
                  pl.BlockSpec((tq, HEAD_DIM), lambda h, p, qb, kb: (qb[p], cq + h)),
                  pl.BlockSpec((tk, HEAD_DIM), lambda h, p, qb, kb: (kb[p], ck + h)),
                  pl.BlockSpec((tk, HEAD_DIM), lambda h, p, qb, kb: (kb[p], cv + h)),
                  pl.BlockSpec(u.shape, lambda h, p, qb, kb: (0, 0))],
        out_specs=pl.BlockSpec((tq, HEAD_DIM), lambda h, p, qb, kb: (qb[p], h)),
        scratch_shapes=[pltpu.VMEM((tq, LANE), F32), pltpu.VMEM((tq, HEAD_DIM), F32)])
    return pl.pallas_call(
        functools.partial(_sbp_body, tq=tq, tk=tk, rc=rc),
        grid_spec=grid_spec,
        out_shape=jax.ShapeDtypeStruct((l, N_HEADS * HEAD_DIM), BF16),
        compiler_params=_cparams(2),
        name="sb_prompt",
    )(jnp.asarray(qb), jnp.asarray(kb), bias, proj, proj, proj, u)


def _sbs_suffix_matrix():
    s = np.arange(PAGE + 16)[:, None]
    j = np.arange(PAGE)[None, :]
    return jnp.asarray(np.where(s < PAGE, j > s, True), BF16)


def _sbs_page(k_heads, v_heads, qrows, bias_row, valid, u, carry, l):
    z = None
    for h in range(N_HEADS):
        zh = _dot_nt(k_heads[h], qrows[h])
        z = zh if z is None else z + zh
    z = z * (HEAD_DIM ** -0.5) + bias_row
    sp = _softplus(z)
    spm = sp if valid is None else jnp.where(valid, sp, 0.0)
    r = _dot(u, spm.astype(BF16))
    w = jnp.exp((z - sp) - (r[:PAGE] + carry[0:1, :]))
    if valid is not None:
        w = jnp.where(valid, w, 0.0)
    wt = jnp.transpose(w)
    pv = [_dot(wt[l * h:l * (h + 1), :].astype(BF16), v_heads[h]) for h in range(N_HEADS)]
    return pv, carry + r[PAGE:PAGE + 8]


def _sbs_group_matrix():
    n = PAGE * N_HEADS
    r = np.arange(n + 16)[:, None]
    c = np.arange(n)[None, :]
    return jnp.asarray(np.where(r < n, c // N_HEADS > r // N_HEADS, True), BF16)


def _lane_groups_sum(x, width):
    out = x
    for g in range(1, LANE // width):
        out = out + pltpu.roll(x, g * width, 1)
    return out


def _sbs_group(kpages, vpages, qbd, bias_g, match, ug, carry, cp):
    n = PAGE * N_HEADS
    kcat = jnp.concatenate([kp[...].astype(BF16) for kp in kpages], axis=1)
    z = _dot(kcat, qbd) * (HEAD_DIM ** -0.5) + bias_g
    sp = _softplus(z)
    r = _dot(ug, jnp.where(match, sp, 0.0).astype(BF16))
    tot = r[n:n + 8]
    lane = lax.broadcasted_iota(jnp.int32, tot.shape, 1)
    newer_pages = jnp.zeros(tot.shape, F32)
    for g in range(1, LANE // cp):
        newer_pages = newer_pages + jnp.where(lane >= g * cp, pltpu.roll(tot, g * cp, 1), 0.0)
    w = jnp.where(match, jnp.exp((z - sp) - (r[:n] + (carry + newer_pages)[0:1, :])), 0.0)
    wt = jnp.transpose(w)
    pv = None
    for p, vp in enumerate(vpages):
        t = _dot(wt[cp * p:cp * (p + 1), :].astype(BF16), vp[...].astype(BF16))
        pv = t if pv is None else pv + t
    return pv, carry + _lane_groups_sum(tot, cp)


def _sbs_body(*refs, l, pps):
    pt_ref, bias_ref, q_ref, kn_ref, vn_ref = refs[:5]
    kpages = refs[5:5 + pps]
    vpages = refs[5 + pps:5 + 2 * pps]
    u_ref, ug_ref, o_ref, carry, acc, qbd, kp, vp = refs[5 + 2 * pps:]
    s = pl.program_id(1)
    n_steps = pl.num_programs(1)
    cp = N_HEADS * l
    group = LANE // cp
    shift = l.bit_length() - 1
    n = PAGE * N_HEADS
    lane = lax.broadcasted_iota(jnp.int32, (1, LANE), 1)
    pair_head = lax.shift_right_logical(lane & (cp - 1), shift)
    bias_g = jnp.zeros((1, LANE), F32)
    for h in range(N_HEADS):
        bias_g = jnp.where(pair_head == h, bias_ref[h], bias_g)

    @pl.when(s == 0)
    def _():
        row_head = lax.shift_right_logical(lax.broadcasted_iota(jnp.int32, (LANE, HEAD_DIM), 0), shift)
        qrows = []
        for h in range(N_HEADS):
            qt = jnp.concatenate([q_ref[:, HEAD_DIM * h:HEAD_DIM * (h + 1)]] * (LANE // l), axis=0)
            qrows.append(jnp.where(row_head == h, qt, 0.0))
        qall_t = _lane_groups_sum(jnp.transpose(sum(qrows[1:], qrows[0])), cp)
        lane_group = lax.shift_right_logical(lax.broadcasted_iota(jnp.int32, (HEAD_DIM, LANE), 1),
                                             cp.bit_length() - 1)
        for p in range(group):
            qbd[HEAD_DIM * p:HEAD_DIM * (p + 1), :] = jnp.where(lane_group == p, qall_t, 0.0).astype(BF16)
        kp[...] = jnp.zeros(kp.shape, F32)
        vp[...] = jnp.zeros(vp.shape, F32)
        kp[0:l, :] = kn_ref[...]
        vp[0:l, :] = vn_ref[...]
        key = lax.broadcasted_iota(jnp.int32, (PAGE, LANE), 0)
        cidx = lax.broadcasted_iota(jnp.int32, (PAGE, LANE), 1)
        valid = key < (cidx & (l - 1))
        heads = lambda ref: [ref[:, HEAD_DIM * h:HEAD_DIM * (h + 1)].astype(BF16) for h in range(N_HEADS)]
        pv, cnew = _sbs_page(heads(kp), heads(vp), [q.astype(BF16) for q in qrows], bias_g, valid,
                             u_ref[...], jnp.zeros(carry.shape, F32), l)
        for h in range(N_HEADS):
            acc[l * h:l * (h + 1), :] = pv[h]
        carry[...] = _lane_groups_sum(jnp.where(lane < cp, cnew, 0.0), cp)

    row = lax.broadcasted_iota(jnp.int32, (n, LANE), 0)
    match = (row & (N_HEADS - 1)) == lax.shift_right_logical(
        lax.broadcasted_iota(jnp.int32, (n, LANE), 1) & (cp - 1), shift)
    c_run = carry[...]
    a_run = acc[...]
    for gi in range(pps // group):
        sl = slice(gi * group, (gi + 1) * group)
        pv, c_run = _sbs_group(kpages[sl], vpages[sl], qbd[...], bias_g, match, ug_ref[...], c_run, cp)
        a_run = a_run + pv
    acc[...] = a_run
    carry[...] = c_run

    @pl.when(s == n_steps - 1)
    def _():
        for h in range(N_HEADS):
            o_ref[:, HEAD_DIM * h:HEAD_DIM * (h + 1)] = acc[l * h:l * (h + 1), :].astype(o_ref.dtype)


def _sb_sample(proj, off, b, l, bias, cache_k, cache_v, layer, page_table, pps=16):
    n_pages = page_table.shape[1]
    cp = N_HEADS * l
    group = LANE // cp
    pps = min(pps, n_pages)
    assert l & (l - 1) == 0 and LANE % cp == 0 and n_pages % pps == 0 and pps % group == 0
    bw = N_HEADS * HEAD_DIM
    u = _sbs_suffix_matrix()
    ug = _sbs_group_matrix()
    cq, ckn, cvn = (off[k] // bw for k in ("sq", "sk", "sv"))
    pages = lambda c: c.reshape(c.shape[0], c.shape[1], PAGE * N_HEADS, HEAD_DIM)

    def page_spec(pp):
        def imap(bi, s, pt):
            page = n_pages - 1 - (s * pps + pp)
            return (layer, pt[bi * n_pages + page], 0, 0)
        return pl.BlockSpec((None, None, PAGE * N_HEADS, HEAD_DIM), imap)

    in_specs = ([pl.BlockSpec(memory_space=pltpu.SMEM),
                 pl.BlockSpec((l, bw), lambda bi, s, pt: (bi, cq)),
                 pl.BlockSpec((l, bw), lambda bi, s, pt: (bi, ckn)),
                 pl.BlockSpec((l, bw), lambda bi, s, pt: (bi, cvn))]
                + [page_spec(pp) for pp in range(pps)] * 2
                + [pl.BlockSpec(u.shape, lambda bi, s, pt: (0, 0)),
                   pl.BlockSpec(ug.shape, lambda bi, s, pt: (0, 0))])
    grid_spec = pltpu.PrefetchScalarGridSpec(
        num_scalar_prefetch=1,
        grid=(b, n_pages // pps),
        in_specs=in_specs,
        out_specs=pl.BlockSpec((l, bw), lambda bi, s, pt: (bi, 0)),
        scratch_shapes=[pltpu.VMEM((8, LANE), F32), pltpu.VMEM((cp, HEAD_DIM), F32),
                        pltpu.VMEM((group * HEAD_DIM, LANE), BF16),
                        pltpu.VMEM((PAGE, bw), F32), pltpu.VMEM((PAGE, bw), F32)])
    return pl.pallas_call(
        functools.partial(_sbs_body, l=l, pps=pps),
        grid_spec=grid_spec,
        out_shape=jax.ShapeDtypeStruct((b * l, bw), BF16),
        compiler_params=_cparams(2),
        name="sb_sample",
    )(page_table.reshape(-1), bias, proj, proj, proj, *([pages(cache_k)] * pps), *([pages(cache_v)] * pps), u, ug)


CONV_HALO = 32


def _conv_body(*refs, g, r, n, has_buf):
    if has_buf:
        ca, cb, buf, w_ref, b_ref, lg_ref, lb_ref, o_ref, new_ref, xp = refs
    else:
        ca, cb, w_ref, b_ref, lg_ref, lb_ref, o_ref, new_ref, xp = refs
    i = pl.program_id(1)
    nh = CONV_W - 1
    ch = ca.shape[-1]

    @pl.when(i == 0)
    def _():
        xp[:, 0:CONV_HALO, :] = jnp.zeros((g, CONV_HALO, ch), F32)
        if has_buf:
            xp[:, CONV_HALO - nh:CONV_HALO, :] = buf[...]

    glu = ca[...] * jax.nn.sigmoid(cb[...])
    xp[:, CONV_HALO:CONV_HALO + r, :] = glu.reshape(g, r, ch)
    base = CONV_HALO - nh
    y = jnp.zeros((g, r, ch), F32) + b_ref[...]
    for j in range(CONV_W):
        y = y + w_ref[j:j + 1, :] * xp[:, base + j:base + j + r, :]
    mu = jnp.mean(y, axis=-1, keepdims=True)
    yc = y - mu
    yn = yc * lax.rsqrt(jnp.mean(yc * yc, axis=-1, keepdims=True) + EPS) * lg_ref[...] + lb_ref[...]
    o_ref[...] = _silu(yn).reshape(g * r, ch).astype(o_ref.dtype)
    tail = xp[:, base + r:CONV_HALO + r, :]
    xp[:, base:CONV_HALO, :] = tail

    @pl.when(i == n - 1)
    def _():
        new_ref[...] = tail


def _conv_branch(proj, off, b, l, buf, layer, w, bias, ln_g, ln_b, tm=512):
    ch = w.shape[1]
    g, r, nt, _ = _row_tiling(b, l, tm)
    n = nt if b == 1 else 1
    ja, jb = off["ca"] // ch, off["cb"] // ch
    rowblk = (lambda bi, i: i) if b == 1 else (lambda bi, i: bi)
    grid = (1, nt) if b == 1 else (nt, 1)
    in_specs = [pl.BlockSpec((g * r, ch), lambda bi, i: (rowblk(bi, i), ja)),
                pl.BlockSpec((g * r, ch), lambda bi, i: (rowblk(bi, i), jb))]
    args = [proj, proj]
    if buf is not None:
        in_specs.append(pl.BlockSpec((None, g, CONV_W - 1, ch), lambda bi, i: (layer, bi, 0, 0)))
        args.append(buf)
    const2 = lambda a: pl.BlockSpec(a.shape, lambda bi, i: (0, 0))
    small = [w, bias.reshape(1, ch), ln_g.reshape(1, ch), ln_b.reshape(1, ch)]
    in_specs += [const2(a) for a in small]
    args += small
    return pl.pallas_call(
        functools.partial(_conv_body, g=g, r=r, n=n, has_buf=buf is not None),
        grid=grid,
        in_specs=in_specs,
        out_specs=[pl.BlockSpec((g * r, ch), lambda bi, i: (rowblk(bi, i), 0)),
                   pl.BlockSpec((g, CONV_W - 1, ch), lambda bi, i: (bi, 0, 0))],
        out_shape=[jax.ShapeDtypeStruct((b * l, ch), BF16),
                   jax.ShapeDtypeStruct((b, CONV_W - 1, ch), F32)],
        scratch_shapes=[pltpu.VMEM((g, CONV_HALO + r, ch), F32)],
        compiler_params=_cparams(2),
        name="conv_branch",
    )(*args)


def _gla_consts():
    kd = N_HEADS * GLA_DK
    sel = (np.arange(kd)[:, None] // GLA_DK) == (np.arange(N_HEADS * HEAD_DIM)[None, :] // HEAD_DIM)
    tri = np.arange(GLA_SUB)[:, None] >= np.arange(GLA_SUB)[None, :]
    return jnp.asarray(sel, BF16), jnp.asarray(tri, BF16)


def _gla_sub(q, k, v, gt, glr, wlr, blr, sel, tri, st, pad):
    c = GLA_SUB
    kd = N_HEADS * GLA_DK
    x = _dot(glr.astype(BF16), wlr) + blr
    la = -_softplus(-x) * (1.0 / GLA_TAU)
    row = lax.broadcasted_iota(jnp.int32, (c, kd), 0)
    if pad:
        la = jnp.where(row >= pad, la, 0.0)
    hi, lo = _split_bf16(la)
    bc = _dot(tri, hi) + _dot(tri, lo)
    qs = q * (GLA_DK ** -0.5)
    blocks = []
    for s in range(pad, c):
        e = jnp.where(row >= s, jnp.exp(jnp.minimum(bc - bc[s:s + 1, :], 0.0)), 0.0)
        blocks.append((qs * k[s:s + 1, :] * e).astype(BF16))
    attb = _dot(jnp.concatenate(blocks, axis=0), sel)
    od = jnp.zeros((c, N_HEADS * HEAD_DIM), F32)
    for n_, s in enumerate(range(pad, c)):
        od = od + attb[n_ * c:(n_ + 1) * c, :] * v[s:s + 1, :]
    blast = bc[c - 1:c, :]
    qe = qs * jnp.exp(bc)
    ke = k * jnp.exp(blast - bc)
    lane_head = lax.broadcasted_iota(jnp.int32, (c, kd), 1) // GLA_DK
    q_st = jnp.concatenate([jnp.where(lane_head == h, qe, 0.0) for h in range(N_HEADS)], axis=0)
    k_st = jnp.concatenate([jnp.where(lane_head == h, ke, 0.0) for h in range(N_HEADS)], axis=0)
    v_st = jnp.concatenate([v[:, HEAD_DIM * h:HEAD_DIM * (h + 1)] for h in range(N_HEADS)], axis=0)
    oi = _dot_nt(q_st.astype(BF16), st.astype(BF16))
    st_new = jnp.exp(blast) * st + _dot_tn(v_st.astype(BF16), k_st.astype(BF16))
    o = od + jnp.concatenate([oi[c * h:c * (h + 1), :] for h in range(N_HEADS)], axis=1)
    outs = []
    for h in range(N_HEADS):
        oh = o[:, HEAD_DIM * h:HEAD_DIM * (h + 1)]
        outs.append(oh * lax.rsqrt(jnp.mean(oh * oh, axis=-1, keepdims=True) + EPS))
    return jnp.concatenate(outs, axis=1) * _silu(gt), st_new


def _gla_body(*refs, rb, n, has_s0):
    if has_s0:
        gq, gk, gv, gg, glr, wlr, blr, sel, tri, s0, o_ref, sn_ref, st = refs
    else:
        gq, gk, gv, gg, glr, wlr, blr, sel, tri, o_ref, sn_ref, st = refs
    i = pl.program_id(1)

    @pl.when(i == 0)
    def _():
        if has_s0:
            st[...] = s0[0]
        else:
            st[...] = jnp.zeros(st.shape, F32)

    c = GLA_SUB
    if rb < c:
        pad = c - rb
        zp = lambda a: jnp.concatenate([jnp.zeros((pad, a.shape[1]), F32), a], axis=0)
        o, st_new = _gla_sub(zp(gq[...]), zp(gk[...]), zp(gv[...]), zp(gg[...]), zp(glr[...]),
                             wlr[...], blr[...], sel[...], tri[...], st[...], pad)
        o_ref[...] = o[pad:, :].astype(o_ref.dtype)
        st[...] = st_new
    else:
        for m in range(rb // c):
            rs = slice(m * c, (m + 1) * c)
            o, st_new = _gla_sub(gq[rs, :], gk[rs, :], gv[rs, :], gg[rs, :], glr[rs, :],
                                 wlr[...], blr[...], sel[...], tri[...], st[...], 0)
            o_ref[rs, :] = o.astype(o_ref.dtype)
            st[...] = st_new

    @pl.when(i == n - 1)
    def _():
        sn_ref[0] = st[...]


def _gla(proj, off, b, l, wlr, blr, s0t, layer, rb=128):
    rb = min(rb, l)
    n = l // rb
    kd = N_HEADS * GLA_DK
    bw = N_HEADS * HEAD_DIM
    sel, tri = _gla_consts()

    def pspec(name, width):
        j = off[name] // width
        return pl.BlockSpec((rb, width), lambda bi, i: (bi * n + i, j))

    const2 = lambda a: pl.BlockSpec(a.shape, lambda bi, i: (0, 0))
    in_specs = [pspec("gq", kd), pspec("gk", kd), pspec("gv", bw), pspec("gg", bw), pspec("glr", LANE),
                const2(wlr), const2(blr), const2(sel), const2(tri)]
    args = [proj, proj, proj, proj, proj, wlr, blr, sel, tri]
    if s0t is not None:
        in_specs.append(pl.BlockSpec((None, 1, HEAD_DIM, kd), lambda bi, i: (layer, bi, 0, 0)))
        args.append(s0t)
    return pl.pallas_call(
        functools.partial(_gla_body, rb=rb, n=n, has_s0=s0t is not None),
        grid=(b, n),
        in_specs=in_specs,
        out_specs=[pl.BlockSpec((rb, bw), lambda bi, i: (bi * n + i, 0)),
                   pl.BlockSpec((1, HEAD_DIM, kd), lambda bi, i: (bi, 0, 0))],
        out_shape=[jax.ShapeDtypeStruct((b * l, bw), BF16),
                   jax.ShapeDtypeStruct((b, HEAD_DIM, kd), F32)],
        scratch_shapes=[pltpu.VMEM((HEAD_DIM, kd), F32)],
        compiler_params=_cparams(2),
        name="gla",
    )(*args)


def _merge_body(a_ref, b_ref, c_ref, d_ref, g0, g1, g2, g3, w_ref, o_ref):
    acc = None
    bw = a_ref.shape[1]
    for i, (br, gl) in enumerate(((a_ref, g0), (b_ref, g1), (c_ref, g2), (d_ref, g3))):
        t = jax.nn.sigmoid(gl[...]) * _dot(br[...], w_ref[bw * i:bw * (i + 1), :])
        acc = t if acc is None else acc + t
    o_ref[...] = acc.astype(o_ref.dtype)


def _merge(branches, proj, wb, layer, tm=512, tn=1024):
    rows, bw = branches[0].shape
    d = wb.shape[2]
    tm = min(tm, rows)
    nj = d // tn
    br_spec = pl.BlockSpec((tm, bw), lambda i, j: (i, 0))
    gate_specs = [pl.BlockSpec((tm, tn), (lambda i, j, q=q: (i, q * nj + j))) for q in range(4)]
    return pl.pallas_call(
        _merge_body,
        grid=(rows // tm, nj),
        in_specs=[br_spec] * 4 + gate_specs + [pl.BlockSpec((None, 4 * bw, tn), lambda i, j: (layer, 0, j))],
        out_specs=pl.BlockSpec((tm, tn), lambda i, j: (i, j)),
        out_shape=jax.ShapeDtypeStruct((rows, d), BF16),
        compiler_params=_cparams(2),
        name="merge",
    )(*branches, proj, proj, proj, proj, wb)


def _resid_body(a_ref, w_ref, x_ref, g_ref, o_ref):
    y = _dot(a_ref[...], w_ref[...])
    o_ref[...] = x_ref[...] + g_ref[...] * y.reshape(o_ref.shape)


def _resid_proj(a, w, layer, x, gate, tm=1024, tn=512):
    b, l, d = x.shape
    k = a.shape[1]
    g, r, nt, xmap = _row_tiling(b, l, tm)
    gmap = (lambda i: (0, 0)) if b == 1 else (lambda i: (i, 0))
    return pl.pallas_call(
        _resid_body,
        grid=(nt, d // tn),
        in_specs=[pl.BlockSpec((g * r, k), lambda i, j: (i, 0)),
                  pl.BlockSpec((None, k, tn), lambda i, j: (layer, 0, j)),
                  pl.BlockSpec((g, r, tn), lambda i, j: xmap(i)[:2] + (j,)),
                  pl.BlockSpec((g, 1, tn), lambda i, j: gmap(i) + (j,))],
        out_specs=pl.BlockSpec((g, r, tn), lambda i, j: xmap(i)[:2] + (j,)),
        out_shape=jax.ShapeDtypeStruct((b, l, d), F32),
        compiler_params=_cparams(2),
        name="resid_proj",
    )(a, w, x, gate)


FFN_HALO = 8


def _ffn_body(*refs, g, r, has_buf):
    if has_buf:
        (x_ref, n_ref, sh_ref, sc_ref, wa_ref, wb_ref, cw_ref, cb_ref, buf_ref,
         y_ref, new_ref, h_ref, ap, carry) = refs
    else:
        (x_ref, n_ref, sh_ref, sc_ref, wa_ref, wb_ref, cw_ref, cb_ref,
         y_ref, new_ref, h_ref, ap, carry) = refs
    i = pl.program_id(0)
    j = pl.program_id(1)
    nh = FFN_CONV_W - 1
    tn = wa_ref.shape[1]

    @pl.when(j == 0)
    def _():
        h = _modulate(x_ref[...], n_ref[...], sh_ref[...], sc_ref[...])
        h_ref[...] = h.reshape(h_ref.shape).astype(BF16)

    hb = h_ref[...]
    a3 = _dot(hb, wa_ref[...]).reshape(g, r, tn)
    b3 = _dot(hb, wb_ref[...]).reshape(g, r, tn)
    if has_buf:
        halo = buf_ref[...]
    else:
        halo = jnp.where(i == 0, 0.0, carry[j])
    ap[:, FFN_HALO - nh:FFN_HALO, :] = halo
    ap[:, FFN_HALO:FFN_HALO + r, :] = a3
    conv = (cb_ref[...] + cw_ref[0:1, :] * ap[:, FFN_HALO - 2:FFN_HALO - 2 + r, :]
            + cw_ref[1:2, :] * ap[:, FFN_HALO - 1:FFN_HALO - 1 + r, :] + cw_ref[2:3, :] * a3)
    y_ref[...] = (_silu(conv) * b3).reshape(g * r, tn).astype(y_ref.dtype)
    tail = a3[:, r - nh:r, :]
    new_ref[...] = tail
    if not has_buf:
        carry[j] = tail


def _ffn_up(x, ng, sh, sc, w_up, layer, cw, cb, buf, tm=1024, tn=512):
    b, l, d = x.shape
    f = w_up.shape[2] // 2
    g, r, nt, xmap = _row_tiling(b, l, tm)
    gmap = (lambda i: (0, 0, 0)) if b == 1 else (lambda i: (i, 0, 0))
    nh = FFN_CONV_W - 1
    nj = f // tn
    in_specs = [pl.BlockSpec((g, r, d), lambda i, j: xmap(i)),
                pl.BlockSpec((1, d), lambda i, j: (0, 0)),
                pl.BlockSpec((g, 1, d), lambda i, j: gmap(i)),
                pl.BlockSpec((g, 1, d), lambda i, j: gmap(i)),
                pl.BlockSpec((None, d, tn), lambda i, j: (layer, 0, j)),
                pl.BlockSpec((None, d, tn), lambda i, j: (layer, 0, nj + j)),
                pl.BlockSpec((FFN_CONV_W, tn), lambda i, j: (0, j)),
                pl.BlockSpec((1, tn), lambda i, j: (0, j))]
    args = [x, ng.reshape(1, d), sh, sc, w_up, w_up, cw, cb.reshape(1, f)]
    if buf is not None:
        in_specs.append(pl.BlockSpec((None, g, nh, tn), lambda i, j: (layer, i, 0, j)))
        args.append(buf)
    n_slots = nt if b == 1 else 1
    newmap = (lambda i, j: (i, 0, 0, j)) if b == 1 else (lambda i, j: (0, i, 0, j))
    return pl.pallas_call(
        functools.partial(_ffn_body, g=g, r=r, has_buf=buf is not None),
        grid=(nt, nj),
        in_specs=in_specs,
        out_specs=[pl.BlockSpec((g * r, tn), lambda i, j: (i, j)),
                   pl.BlockSpec((None, g, nh, tn), newmap)],
        out_shape=[jax.ShapeDtypeStruct((b * l, f), BF16),
                   jax.ShapeDtypeStruct((n_slots, b, nh, f), F32)],
        scratch_shapes=[pltpu.VMEM((g * r, d), BF16),
                        pltpu.VMEM((g, FFN_HALO + r, tn), F32),
                        pltpu.VMEM((nj, g, nh, tn), F32)],
        compiler_params=_cparams(2),
        name="ffn_up",
    )(*args)


def _final_body(x_ref, g_ref, o_ref):
    x = x_ref[...]
    o_ref[...] = x * lax.rsqrt(jnp.mean(x * x, axis=-1, keepdims=True) + EPS) * g_ref[...]


def _final_norm(x, g, tm=512):
    b, l, d = x.shape
    gg, r, nt, xmap = _row_tiling(b, l, tm)
    return pl.pallas_call(
        _final_body,
        grid=(nt,),
        in_specs=[pl.BlockSpec((gg, r, d), xmap), pl.BlockSpec((1, d), lambda i: (0, 0))],
        out_specs=pl.BlockSpec((gg, r, d), xmap),
        out_shape=jax.ShapeDtypeStruct((b, l, d), F32),
        compiler_params=_cparams(1),
        name="final_norm",
    )(x, g.reshape(1, d))


def _cast_body(x_ref, o_ref):
    o_ref[...] = x_ref[...].astype(o_ref.dtype)


def _cast_stack(w, tr=512):
    depth, r, c = w.shape
    spec = pl.BlockSpec((1, tr, c), lambda l, i: (l, i, 0))
    return pl.pallas_call(
        _cast_body, grid=(depth, r // tr), in_specs=[spec], out_specs=spec,
        out_shape=jax.ShapeDtypeStruct(w.shape, BF16), compiler_params=_cparams(2), name="cast_w",
    )(w)


def _prep_up_body(x_ref, o_ref):
    f = x_ref.shape[2]
    o_ref[0, :, :f] = x_ref[0].astype(BF16)
    o_ref[0, :, f:] = jnp.zeros((o_ref.shape[1], o_ref.shape[2] - f), BF16)


def _prep_w_up(w_up, f_pad, tr=256):
    depth, d, n = w_up.shape
    d_ff = n // 2
    return pl.pallas_call(
        _prep_up_body, grid=(depth, d // tr, 2),
        in_specs=[pl.BlockSpec((1, tr, d_ff), lambda l, i, h: (l, i, h))],
        out_specs=pl.BlockSpec((1, tr, f_pad), lambda l, i, h: (l, i, h)),
        out_shape=jax.ShapeDtypeStruct((depth, d, 2 * f_pad), BF16),
        compiler_params=_cparams(3), name="prep_w_up",
    )(w_up)


def _prep_down_body(x_ref, o_ref):
    f = x_ref.shape[1]
    o_ref[0, :f, :] = x_ref[0].astype(BF16)
    o_ref[0, f:, :] = jnp.zeros((o_ref.shape[1] - f, o_ref.shape[2]), BF16)


def _prep_w_down(w_down, f_pad, tc=256):
    depth, d_ff, d = w_down.shape
    return pl.pallas_call(
        _prep_down_body, grid=(depth, d // tc),
        in_specs=[pl.BlockSpec((1, d_ff, tc), lambda l, j: (l, 0, j))],
        out_specs=pl.BlockSpec((1, f_pad, tc), lambda l, j: (l, 0, j)),
        out_shape=jax.ShapeDtypeStruct((depth, f_pad, d), BF16),
        compiler_params=_cparams(2), name="prep_w_down",
    )(w_down)


PREP_TR = 1024


def _prep_in_body(main_ref, extra_ref, o_ref, *, n_gate, n_core):
    j = pl.program_id(1)

    @pl.when(j < n_gate)
    def _():
        o_ref[0] = jnp.concatenate([main_ref[0, GLA_RANK:, :], extra_ref[0]], axis=0).astype(BF16)

    @pl.when((j >= n_gate) & (j < n_gate + n_core))
    def _():
        o_ref[0] = main_ref[0].astype(BF16)

    @pl.when(j == n_gate + n_core)
    def _():
        row = lax.broadcasted_iota(jnp.int32, main_ref.shape[1:], 0)
        o_ref[0] = jnp.where(row < GLA_RANK, main_ref[0], 0.0).astype(BF16)


def _prep_w_in(w_in_t, off):
    depth, n_in, d = w_in_t.shape
    core = off["glr"] - off["rq"]
    assert core % PREP_TR == 0 and (4 * d) % PREP_TR == 0 and n_in == core + GLA_RANK + 4 * d
    n_gate, n_core = 4 * d // PREP_TR, core // PREP_TR
    c0 = core // PREP_TR

    def main_map(l, j):
        return (l, jnp.where(j < n_gate, c0 + j, jnp.where(j < n_gate + n_core, j - n_gate, c0)), 0)

    def extra_map(l, j):
        return (l, jnp.where(j < n_gate, (c0 + j + 1) * (PREP_TR // GLA_RANK), 0), 0)

    return pl.pallas_call(
        functools.partial(_prep_in_body, n_gate=n_gate, n_core=n_core),
        grid=(depth, n_gate + n_core + 1),
        in_specs=[pl.BlockSpec((1, PREP_TR, d), main_map), pl.BlockSpec((1, GLA_RANK, d), extra_map)],
        out_specs=pl.BlockSpec((1, PREP_TR, d), lambda l, j: (l, j, 0)),
        out_shape=jax.ShapeDtypeStruct((depth, off["total"], d), BF16),
        compiler_params=_cparams(2), name="prep_w_in",
    )(w_in_t, w_in_t)


def _pad_cols(a, n):
    return jnp.pad(a, [(0, 0)] * (a.ndim - 1) + [(0, n - a.shape[-1])])


def _layer(x, mods, cos, sin, layer, states, sb_cache, sw, lw, off):
    b, l, d = x.shape
    sh1, sc1, g1, sh2, sc2, g2 = mods
    ret_s0, conv_buf, gla_s0t, ffn_buf = states if states is not None else (None,) * 4
    proj = _modproj(x, lw["norm1_g"], sh1, sc1, sw["w_in"], layer, tm=1024, tn=768)

    o_a, ret_new = _retention(proj, off, b, l, cos, sin, ret_s0, layer)
    if sb_cache is None:
        o_b = _sb_prompt(proj, off, l, lw["sb_bias"])
    else:
        cache_k, cache_v, page_table = sb_cache
        o_b = _sb_sample(proj, off, b, l, lw["sb_bias"], cache_k, cache_v, layer, page_table)
    o_c, conv_new = _conv_branch(proj, off, b, l, conv_buf, layer, lw["conv_w"], lw["conv_b"],
                                 lw["conv_ln_g"], lw["conv_ln_b"])
    o_d, gla_t = _gla(proj, off, b, l, lw["gla_w_lr"], lw["gla_b_lr"], gla_s0t, layer)
    gla_new = gla_t.reshape(b, HEAD_DIM, N_HEADS, GLA_DK).transpose(0, 2, 3, 1)

    merged = _merge((o_a, o_b, o_c, o_d), proj, sw["w_branch"], layer)
    x1 = _resid_proj(merged, sw["w_out"], layer, x, g1)

    y, ffn_slots = _ffn_up(x1, lw["norm2_g"], sh2, sc2, sw["w_up"], layer,
                           lw["ffn_conv_w"], lw["ffn_conv_b"], ffn_buf)
    x2 = _resid_proj(y, sw["w_down"], layer, x1, g2)

    bw = d // 4
    sk = proj[:, off["sk"]:off["sk"] + bw].reshape(b, l, N_HEADS, HEAD_DIM)
    sv = proj[:, off["sv"]:off["sv"] + bw].reshape(b, l, N_HEADS, HEAD_DIM)
    return x2, (sk, sv, ret_new, conv_new, gla_new, ffn_slots[-1])


def kernel(x_prompt, x_sample, cache_sb_k, cache_sb_v, page_table, state_ret, state_conv, state_gla,
           state_ffn_conv, c_prompt, c_sample, norm1_g, norm2_g, w_ada, b_ada, w_in, gla_w_lr, gla_b_lr,
           sb_bias, conv_w, conv_b, conv_ln_g, conv_ln_b, w_branch, w_out, w_up, ffn_conv_w, ffn_conv_b,
           w_down, final_g):
    bp, lp, d = x_prompt.shape
    bs, ls, _ = x_sample.shape
    depth = w_in.shape[0]
    d_ff = w_down.shape[1]
    f_pad = -(-d_ff // 512) * 512
    past_len = page_table.shape[1] * PAGE
    off = _layout(d)

    n_c = bp + bs
    m_pad = -(-n_c // 8) * 8
    c_all = jnp.concatenate([c_prompt, c_sample, jnp.zeros((m_pad - n_c, d), F32)], axis=0)
    mod = _ada(c_all, w_ada, b_ada)

    cos_p, sin_p = _rope_tables(jnp.arange(lp))
    cos_s, sin_s = _rope_tables(past_len + jnp.arange(ls))

    sw = {
        "w_in": _prep_w_in(jnp.swapaxes(w_in, 1, 2), off),
        "w_branch": _cast_stack(w_branch.reshape(depth, -1, d)),
        "w_out": _cast_stack(w_out),
        "w_up": _prep_w_up(w_up, f_pad),
        "w_down": _prep_w_down(w_down, f_pad),
    }
    gla_t = state_gla.transpose(0, 1, 4, 2, 3).reshape(depth, bs, HEAD_DIM, N_HEADS * GLA_DK)
    states_s = (state_ret, state_conv, gla_t, _pad_cols(state_ffn_conv, f_pad))
    sb_cache = (cache_sb_k, cache_sb_v, page_table)

    xp, xs = x_prompt, x_sample
    out_p = [[] for _ in range(6)]
    out_s = [[] for _ in range(6)]
    for l in range(depth):
        lw = {
            "norm1_g": norm1_g[l], "norm2_g": norm2_g[l],
            "gla_w_lr": jnp.pad(gla_w_lr[l], ((0, LANE - GLA_RANK), (0, 0))).astype(BF16),
            "gla_b_lr": gla_b_lr[l].reshape(1, -1),
            "sb_bias": sb_bias[l],
            "conv_w": conv_w[l], "conv_b": conv_b[l], "conv_ln_g": conv_ln_g[l], "conv_ln_b": conv_ln_b[l],
            "ffn_conv_w": _pad_cols(ffn_conv_w[l], f_pad), "ffn_conv_b": _pad_cols(ffn_conv_b[l], f_pad),
        }
        mod_l = mod[l]
        mods_p = tuple(mod_l[:bp, i * d:(i + 1) * d].reshape(bp, 1, d) for i in range(N_ADA))
        mods_s = tuple(mod_l[bp:n_c, i * d:(i + 1) * d].reshape(bs, 1, d) for i in range(N_ADA))

        xp, st_p = _layer(xp, mods_p, cos_p, sin_p, l, None, None, sw, lw, off)
        xs, st_s = _layer(xs, mods_s, cos_s, sin_s, l, states_s, sb_cache, sw, lw, off)
        for i in range(6):
            out_p[i].append(st_p[i])
            out_s[i].append(st_s[i])

    y_prompt = _final_norm(xp, final_g)
    y_sample = _final_norm(xs, final_g)
    kp, vp, rp, cp, gp, fp = [jnp.stack(t, axis=0) for t in out_p]
    ks_, vs_, rs_, cs_, gs_, fs_ = [jnp.stack(t, axis=0) for t in out_s]
    fp = fp[..., :d_ff]
    fs_ = fs_[..., :d_ff]
    return (y_prompt, y_sample, kp, vp, rp, cp, gp, fp, ks_, vs_, rs_, cs_, gs_, fs_)
```

```python
import functools

import numpy as np
import jax
import jax.numpy as jnp
from jax import lax
from jax.experimental import pallas as pl
from jax.experimental.pallas import tpu as pltpu

F32 = jnp.float32
BF16 = jnp.bfloat16

EPS = 1e-6
N_HEADS = 4
HEAD_DIM = 128
GLA_DK = 64
GLA_RANK = 16
GLA_TAU = 16.0
GLA_SUB = 16
ROPE_BASE = 10000.0
PAGE = 128
CONV_W = 31
FFN_CONV_W = 3
N_ADA = 6
LANE = 128
VMEM_LIMIT = 56 * 1024 * 1024

C_GATE = 0


def _layout(d):
    bw = d // 4
    off = {}
    o = 4 * d
    for name, w in (("rq", bw), ("rk", bw), ("rv", bw), ("rg", bw),
                    ("sq", bw), ("sk", bw), ("sv", bw),
                    ("ca", bw), ("cb", bw),
                    ("gq", bw // 2), ("gk", bw // 2), ("gv", bw), ("gg", bw),
                    ("glr", 2 * LANE)):
        off[name] = o
        o += w
    off["total"] = o
    return off


def _cparams(n_axes):
    return pltpu.CompilerParams(dimension_semantics=("arbitrary",) * n_axes,
                                vmem_limit_bytes=VMEM_LIMIT)


def _silu(x):
    return x * jax.nn.sigmoid(x)


LOG2E = 1.4426950408889634


def _softplus(x):
    return jnp.maximum(x, 0.0) + jnp.log(1.0 + jnp.exp2(jnp.abs(x) * (-LOG2E)))


def _split_bf16(x):
    hi = x.astype(BF16)
    lo = (x - hi.astype(F32)).astype(BF16)
    return hi, lo


def _dot(a, b):
    return jnp.dot(a, b, preferred_element_type=F32)


def _dot_nt(a, b):
    return lax.dot_general(a, b, (((1,), (1,)), ((), ())), preferred_element_type=F32)


def _dot_tn(a, b):
    return lax.dot_general(a, b, (((0,), (0,)), ((), ())), preferred_element_type=F32)


def _ada_body(c_ref, w_ref, b_ref, o_ref):
    s = _silu(c_ref[...]).astype(BF16)
    o_ref[0] = _dot(s, w_ref[0].astype(BF16)) + b_ref[0]


def _ada(c_all, w_ada, b_ada):
    depth, d, n = w_ada.shape
    m = c_all.shape[0]
    tn = 1024
    return pl.pallas_call(
        _ada_body,
        grid=(depth, n // tn),
        in_specs=[pl.BlockSpec((m, d), lambda l, j: (0, 0)),
                  pl.BlockSpec((1, d, tn), lambda l, j: (l, 0, j)),
                  pl.BlockSpec((1, 1, tn), lambda l, j: (l, 0, j))],
        out_specs=pl.BlockSpec((1, m, tn), lambda l, j: (l, 0, j)),
        out_shape=jax.ShapeDtypeStruct((depth, m, n), F32),
        compiler_params=_cparams(2),
        name="ada_mod",
    )(c_all, w_ada, b_ada.reshape(depth, 1, n))


def _modulate(x, g, sh, sc):
    ms = jnp.mean(x * x, axis=-1, keepdims=True)
    y = x * lax.rsqrt(ms + EPS) * g
    return y * (1.0 + sc) + sh


def _modproj_body(x_ref, g_ref, sh_ref, sc_ref, w_ref, o_ref, h_ref):
    @pl.when(pl.program_id(1) == 0)
    def _():
        h = _modulate(x_ref[...], g_ref[...], sh_ref[...], sc_ref[...])
        h_ref[...] = h.reshape(h_ref.shape).astype(BF16)

    o_ref[...] = _dot_nt(h_ref[...], w_ref[...])


def _row_tiling(b, l, tm):
    if b == 1:
        r = min(tm, l)
        return 1, r, l // r, (lambda i: (0, i, 0))
    g = min(max(tm // l, 1), b)
    return g, l, b // g, (lambda i: (i, 0, 0))


def _modproj(x, g, sh, sc, w, layer, tm, tn):
    b, l, d = x.shape
    n = w.shape[1]
    gg, r, nt, xmap = _row_tiling(b, l, tm)
    gmap = (lambda i: (0, 0, 0)) if b == 1 else (lambda i: (i, 0, 0))
    return pl.pallas_call(
        _modproj_body,
        grid=(nt, n // tn),
        in_specs=[pl.BlockSpec((gg, r, d), lambda i, j: xmap(i)),
                  pl.BlockSpec((1, d), lambda i, j: (0, 0)),
                  pl.BlockSpec((gg, 1, d), lambda i, j: gmap(i)),
                  pl.BlockSpec((gg, 1, d), lambda i, j: gmap(i)),
                  pl.BlockSpec((None, tn, d), lambda i, j: (layer, j, 0))],
        out_specs=pl.BlockSpec((gg * r, tn), lambda i, j: (i, j)),
        out_shape=jax.ShapeDtypeStruct((b * l, n), F32),
        scratch_shapes=[pltpu.VMEM((gg * r, d), BF16)],
        compiler_params=_cparams(2),
        name="modproj",
    )(x, g.reshape(1, d), sh, sc, w)


def _ret_body(*refs, c, n, has_s0, cdec):
    if has_s0:
        (rq, rk, rv, rg, cos, sin, dm, qd, kd, s0, o_ref, sn_ref, s_scr, kp, vp) = refs
    else:
        (rq, rk, rv, rg, cos, sin, dm, qd, kd, o_ref, sn_ref, s_scr, kp, vp) = refs
        s0 = None
    i = pl.program_id(1)

    @pl.when(i == 0)
    def _():
        if has_s0:
            s_scr[...] = s0[0]
        else:
            s_scr[...] = jnp.zeros(s_scr.shape, F32)
        if c < LANE:
            kp[...] = jnp.zeros(kp.shape, F32)
            vp[...] = jnp.zeros(vp.shape, F32)

    c2 = cos[...]
    s2 = sin[...]
    for h in range(N_HEADS):
        sl = slice(HEAD_DIM * h, HEAD_DIM * (h + 1))
        q = rq[:, sl]
        k = rk[:, sl]
        v = rv[:, sl]
        gt = rg[:, sl]
        qr = q * c2 + pltpu.roll(q, HEAD_DIM // 2, 1) * s2
        kr = (k * c2 + pltpu.roll(k, HEAD_DIM // 2, 1) * s2) * (HEAD_DIM ** -0.5)
        if c < LANE:
            kp[0, 0:c, sl] = kr
            kp[1, 0:c, sl] = kr * kd[h]
            vp[0:c, sl] = v
            kb = kp[0, :, sl].astype(BF16)
            kdb = kp[1, :, sl].astype(BF16)
            vb = vp[:, sl].astype(BF16)
        else:
            kb = kr.astype(BF16)
            vb = v.astype(BF16)
            kdb = (kr * kd[h]).astype(BF16)
        att = _dot_nt(qr.astype(BF16), kb) * dm[h]
        sh_ = s_scr[h]
        o = _dot(att.astype(BF16), vb) + _dot((qr * qd[h]).astype(BF16), sh_.astype(BF16))
        s_scr[h] = cdec[h] * sh_ + _dot_tn(kdb, vb)
        o = o - jnp.mean(o, axis=-1, keepdims=True)
        o = o * lax.rsqrt(jnp.mean(o * o, axis=-1, keepdims=True) + EPS)
        o_ref[:, sl] = (o * _silu(gt)).astype(o_ref.dtype)

    @pl.when(i == n - 1)
    def _():
        sn_ref[0] = s_scr[...]


def _rope_tables(pos):
    half = HEAD_DIM // 2
    inv = ROPE_BASE ** (-jnp.arange(half, dtype=F32) / half)
    ang = pos.astype(F32)[:, None] * inv[None, :]
    cos = jnp.cos(ang)
    sin = jnp.sin(ang)
    return jnp.concatenate([cos, cos], axis=-1), jnp.concatenate([-sin, sin], axis=-1)


def _ret_consts(c):
    ck = max(c, LANE)
    lg = np.log1p(-np.exp2(-5.0 - np.arange(N_HEADS, dtype=np.float64)))
    idx = np.arange(c, dtype=np.float64)
    rel = idx[:, None] - idx[None, :]
    dmask = np.where(rel[None] >= 0, np.exp(np.maximum(rel, 0.0)[None] * lg[:, None, None]), 0.0)
    dm = np.zeros((N_HEADS, c, ck))
    dm[:, :, :c] = dmask
    qd = np.exp((idx[None, :] + 1.0) * lg[:, None])[:, :, None] * np.ones((1, 1, HEAD_DIM))
    kd = np.exp((c - 1.0 - idx)[None, :] * lg[:, None])[:, :, None] * np.ones((1, 1, HEAD_DIM))
    cdec = tuple(float(v) for v in np.exp(c * lg))
    return (jnp.asarray(dm, F32), jnp.asarray(qd, F32), jnp.asarray(kd, F32), cdec)


def _retention(proj, off, b, l, cos, sin, s0, layer):
    c = 128 if l % 128 == 0 else l
    n = l // c
    ck = max(c, LANE)
    dm, qd, kd, cdec = _ret_consts(c)
    bw = N_HEADS * HEAD_DIM
    cb = lambda name: off[name] // bw

    def pspec(name):
        j = cb(name)
        return pl.BlockSpec((c, bw), lambda bi, i: (bi * n + i, j))

    tab = pl.BlockSpec((c, HEAD_DIM), lambda bi, i: (i, 0))
    const3 = lambda a: pl.BlockSpec(a.shape, lambda bi, i: (0, 0, 0))
    in_specs = [pspec("rq"), pspec("rk"), pspec("rv"), pspec("rg"), tab, tab,
                const3(dm), const3(qd), const3(kd)]
    args = [proj, proj, proj, proj, cos, sin, dm, qd, kd]
    if s0 is not None:
        in_specs.append(pl.BlockSpec((None, 1, N_HEADS, HEAD_DIM, HEAD_DIM),
                                     lambda bi, i: (layer, bi, 0, 0, 0)))
        args.append(s0)
    return pl.pallas_call(
        functools.partial(_ret_body, c=c, n=n, has_s0=s0 is not None, cdec=cdec),
        grid=(b, n),
        in_specs=in_specs,
        out_specs=[pl.BlockSpec((c, bw), lambda bi, i: (bi * n + i, 0)),
                   pl.BlockSpec((1, N_HEADS, HEAD_DIM, HEAD_DIM), lambda bi, i: (bi, 0, 0, 0))],
        out_shape=[jax.ShapeDtypeStruct((b * l, bw), BF16),
                   jax.ShapeDtypeStruct((b, N_HEADS, HEAD_DIM, HEAD_DIM), F32)],
        scratch_shapes=[pltpu.VMEM((N_HEADS, HEAD_DIM, HEAD_DIM), F32),
                        pltpu.VMEM((2, ck, bw), F32), pltpu.VMEM((ck, bw), F32)],
        compiler_params=_cparams(2),
        name="retention",
    )(*args)


def _sb_suffix_matrix(tk):
    j = np.arange(tk)[:, None]
    s = np.arange(tk + LANE)[None, :]
    u = np.where(s < tk, j > s, True)
    return jnp.asarray(u, BF16)


def _sb_scores(q, k, bias, valid, u):
    z = _dot_nt(q, k) + bias
    sp = _softplus(z)
    spm = sp if valid is None else jnp.where(valid, sp, 0.0)
    return z - sp, _dot(spm.astype(BF16), u)


def _sb_values(lz, r, v, valid, carry):
    tk = v.shape[0]
    w = jnp.exp(lz - (r[:, :tk] + jnp.concatenate([carry] * (tk // LANE), axis=1)))
    if valid is not None:
        w = jnp.where(valid, w, 0.0)
    return _dot(w.astype(BF16), v), carry + r[:, tk:]


def _sbp_body(qb_ref, kb_ref, bias_ref, q_ref, k_ref, v_ref, u_ref, o_ref, carry, acc, *, tq, tk, rc):
    h = pl.program_id(0)
    p = pl.program_id(1)
    qb = qb_ref[p]
    kb = kb_ref[p]
    ratio = tq // tk
    first_masked = ratio * qb

    @pl.when(kb == first_masked + ratio - 1)
    def _():
        carry[...] = jnp.zeros(carry.shape, F32)
        acc[...] = jnp.zeros(acc.shape, F32)

    def update(masked):
        kbf = k_ref[...].astype(BF16)
        vbf = v_ref[...].astype(BF16)
        chunks = [slice(c * rc, (c + 1) * rc) for c in range(tq // rc)]

        def valid_of(rs):
            if not masked:
                return None
            rows = qb * tq + rs.start + lax.broadcasted_iota(jnp.int32, (rc, tk), 0)
            cols = kb * tk + lax.broadcasted_iota(jnp.int32, (rc, tk), 1)
            return cols < rows

        scores = [_sb_scores((q_ref[rs, :] * (HEAD_DIM ** -0.5)).astype(BF16), kbf, bias_ref[h],
                             valid_of(rs), u_ref[...]) for rs in chunks]
        for rs, (lz, r) in zip(chunks, scores):
            pv, cnew = _sb_values(lz, r, vbf, valid_of(rs), carry[rs, :])
            acc[rs, :] += pv
            carry[rs, :] = cnew

    @pl.when(kb >= first_masked)
    def _():
        update(True)

    @pl.when(kb < first_masked)
    def _():
        update(False)

    @pl.when(kb == 0)
    def _():
        o_ref[...] = acc[...].astype(o_ref.dtype)


def _sb_prompt(proj, off, l, bias, tq=1024, tk=512, rc=512):
    tq, tk = min(tq, l), min(tk, l)
    rc = min(rc, tq)
    nq = l // tq
    ratio = tq // tk
    qb = np.concatenate([np.full(ratio * (i + 1), i) for i in range(nq)]).astype(np.int32)
    kb = np.concatenate([np.arange(ratio * (i + 1) - 1, -1, -1) for i in range(nq)]).astype(np.int32)
    cq, ck, cv = (off[k] // HEAD_DIM for k in ("sq", "sk", "sv"))
    u = _sb_suffix_matrix(tk)
    grid_spec = pltpu.PrefetchScalarGridSpec(
        num_scalar_prefetch=2,
        grid=(N_HEADS, len(qb)),
        in_specs=[pl.BlockSpec(memory_space=pltpu.SMEM),
                  pl.BlockSpec((tq, HEAD_DIM), lambda h, p, qb, kb: (qb[p], cq + h)),
                  pl.BlockSpec((tk, HEAD_DIM), lambda h, p, qb, kb: (kb[p], ck + h)),
                  pl.BlockSpec((tk, HEAD_DIM), lambda h, p, qb, kb: (kb[p], cv + h)),
                  pl.BlockSpec(u.shape, lambda h, p, qb, kb: (0, 0))],
        out_specs=pl.BlockSpec((tq, HEAD_DIM), lambda h, p, qb, kb: (qb[p], h)),
        scratch_shapes=[pltpu.VMEM((tq, LANE), F32), pltpu.VMEM((tq, HEAD_DIM), F32)])
    return pl.pallas_call(
        functools.partial(_sbp_body, tq=tq, tk=tk, rc=rc),
        grid_spec=grid_spec,
        out_shape=jax.ShapeDtypeStruct((l, N_HEADS * HEAD_DIM), BF16),
        compiler_params=_cparams(2),
        name="sb_prompt",
    )(jnp.asarray(qb), jnp.asarray(kb), bias, proj, proj, proj, u)


def _sbs_suffix_matrix():
    s = np.arange(PAGE + 16)[:, None]
    j = np.arange(PAGE)[None, :]
    return jnp.asarray(np.where(s < PAGE, j > s, True), BF16)


def _sbs_page(k_heads, v_heads, qrows, bias_row, valid, u, carry, l):
    z = None
    for h in range(N_HEADS):
        zh = _dot_nt(k_heads[h], qrows[h])
        z = zh if z is None else z + zh
    z = z * (HEAD_DIM ** -0.5) + bias_row
    sp = _softplus(z)
    spm = sp if valid is None else jnp.where(valid, sp, 0.0)
    r = _dot(u, spm.astype(BF16))
    w = jnp.exp((z - sp) - (r[:PAGE] + carry[0:1, :]))
    if valid is not None:
        w = jnp.where(valid, w, 0.0)
    wt = jnp.transpose(w)
    pv = [_dot(wt[l * h:l * (h + 1), :].astype(BF16), v_heads[h]) for h in range(N_HEADS)]
    return pv, carry + r[PAGE:PAGE + 8]


def _sbs_group_matrix():
    n = PAGE * N_HEADS
    r = np.arange(n + 16)[:, None]
    c = np.arange(n)[None, :]
    return jnp.asarray(np.where(r < n, c // N_HEADS > r // N_HEADS, True), BF16)


def _lane_groups_sum(x, width):
    out = x
    for g in range(1, LANE // width):
        out = out + pltpu.roll(x, g * width, 1)
    return out


def _sbs_group_scores(kpages, qbd, bias_g, match, ug):
    kcat = jnp.concatenate([kp[...].astype(BF16) for kp in kpages], axis=1)
    z = _dot(kcat, qbd) * (HEAD_DIM ** -0.5) + bias_g
    sp = _softplus(z)
    return z - sp, _dot(ug, jnp.where(match, sp, 0.0).astype(BF16))


def _sbs_group_values(lz, r, vpages, match, carry, cp):
    n = PAGE * N_HEADS
    tot = r[n:n + 8]
    lane = lax.broadcasted_iota(jnp.int32, tot.shape, 1)
    newer_pages = jnp.zeros(tot.shape, F32)
    for g in range(1, LANE // cp):
        newer_pages = newer_pages + jnp.where(lane >= g * cp, pltpu.roll(tot, g * cp, 1), 0.0)
    w = jnp.where(match, jnp.exp(lz - (r[:n] + (carry + newer_pages)[0:1, :])), 0.0)
    wt = jnp.transpose(w)
    pv = None
    for p, vp in enumerate(vpages):
        t = _dot(wt[cp * p:cp * (p + 1), :].astype(BF16), vp[...].astype(BF16))
        pv = t if pv is None else pv + t
    return pv, carry + _lane_groups_sum(tot, cp)


def _sbs_body(*refs, l, pps):
    pt_ref, bias_ref, q_ref, kn_ref, vn_ref = refs[:5]
    kpages = refs[5:5 + pps]
    vpages = refs[5 + pps:5 + 2 * pps]
    u_ref, ug_ref, o_ref, qbd, kp, vp = refs[5 + 2 * pps:]
    cp = N_HEADS * l
    group = LANE // cp
    shift = l.bit_length() - 1
    n = PAGE * N_HEADS
    lane = lax.broadcasted_iota(jnp.int32, (1, LANE), 1)
    pair_head = lax.shift_right_logical(lane & (cp - 1), shift)
    bias_g = jnp.zeros((1, LANE), F32)
    for h in range(N_HEADS):
        bias_g = jnp.where(pair_head == h, bias_ref[h], bias_g)

    row_head = lax.shift_right_logical(lax.broadcasted_iota(jnp.int32, (LANE, HEAD_DIM), 0), shift)
    qrows = []
    for h in range(N_HEADS):
        qt = jnp.concatenate([q_ref[:, HEAD_DIM * h:HEAD_DIM * (h + 1)]] * (LANE // l), axis=0)
        qrows.append(jnp.where(row_head == h, qt, 0.0))
    qall_t = _lane_groups_sum(jnp.transpose(sum(qrows[1:], qrows[0])), cp)
    lane_group = lax.shift_right_logical(lax.broadcasted_iota(jnp.int32, (HEAD_DIM, LANE), 1),
                                         cp.bit_length() - 1)
    for p in range(group):
        qbd[HEAD_DIM * p:HEAD_DIM * (p + 1), :] = jnp.where(lane_group == p, qall_t, 0.0).astype(BF16)

    row = lax.broadcasted_iota(jnp.int32, (n, LANE), 0)
    match = (row & (N_HEADS - 1)) == lax.shift_right_logical(
        lax.broadcasted_iota(jnp.int32, (n, LANE), 1) & (cp - 1), shift)
    groups = [slice(gi * group, (gi + 1) * group) for gi in range(pps // group)]
    scores = [_sbs_group_scores(kpages[sl], qbd[...], bias_g, match, ug_ref[...]) for sl in groups]

    kp[...] = jnp.zeros(kp.shape, F32)
    vp[...] = jnp.zeros(vp.shape, F32)
    kp[0:l, :] = kn_ref[...]
    vp[0:l, :] = vn_ref[...]
    key = lax.broadcasted_iota(jnp.int32, (PAGE, LANE), 0)
    cidx = lax.broadcasted_iota(jnp.int32, (PAGE, LANE), 1)
    valid = key < (cidx & (l - 1))
    heads = lambda ref: [ref[:, HEAD_DIM * h:HEAD_DIM * (h + 1)].astype(BF16) for h in range(N_HEADS)]
    pv_new, c_new = _sbs_page(heads(kp), heads(vp), [q.astype(BF16) for q in qrows], bias_g, valid,
                              u_ref[...], jnp.zeros((8, LANE), F32), l)
    a_run = jnp.concatenate(pv_new, axis=0)
    c_run = _lane_groups_sum(jnp.where(lane < cp, c_new, 0.0), cp)
    for sl, (lz, r) in zip(groups, scores):
        pv, c_run = _sbs_group_values(lz, r, vpages[sl], match, c_run, cp)
        a_run = a_run + pv
    for h in range(N_HEADS):
        o_ref[:, HEAD_DIM * h:HEAD_DIM * (h + 1)] = a_run[l * h:l * (h + 1), :].astype(o_ref.dtype)


def _sb_sample(proj, off, b, l, bias, cache_k, cache_v, layer, page_table):
    n_pages = page_table.shape[1]
    cp = N_HEADS * l
    group = LANE // cp
    pps = n_pages
    assert l & (l - 1) == 0 and LANE % cp == 0 and pps % group == 0
    bw = N_HEADS * HEAD_DIM
    u = _sbs_suffix_matrix()
    ug = _sbs_group_matrix()
    cq, ckn, cvn = (off[k] // bw for k in ("sq", "sk", "sv"))
    pages = lambda c: c.reshape(c.shape[0], c.shape[1], PAGE * N_HEADS, HEAD_DIM)

    def page_spec(pp):
        return pl.BlockSpec((None, None, PAGE * N_HEADS, HEAD_DIM),
                            lambda bi, pt: (layer, pt[bi * n_pages + n_pages - 1 - pp], 0, 0))

    in_specs = ([pl.BlockSpec(memory_space=pltpu.SMEM),
                 pl.BlockSpec((l, bw), lambda bi, pt: (bi, cq)),
                 pl.BlockSpec((l, bw), lambda bi, pt: (bi, ckn)),
                 pl.BlockSpec((l, bw), lambda bi, pt: (bi, cvn))]
                + [page_spec(pp) for pp in range(pps)] * 2
                + [pl.BlockSpec(u.shape, lambda bi, pt: (0, 0)),
                   pl.BlockSpec(ug.shape, lambda bi, pt: (0, 0))])
    grid_spec = pltpu.PrefetchScalarGridSpec(
        num_scalar_prefetch=1,
        grid=(b,),
        in_specs=in_specs,
        out_specs=pl.BlockSpec((l, bw), lambda bi, pt: (bi, 0)),
        scratch_shapes=[pltpu.VMEM((group * HEAD_DIM, LANE), BF16),
                        pltpu.VMEM((PAGE, bw), F32), pltpu.VMEM((PAGE, bw), F32)])
    return pl.pallas_call(
        functools.partial(_sbs_body, l=l, pps=pps),
        grid_spec=grid_spec,
        out_shape=jax.ShapeDtypeStruct((b * l, bw), BF16),
        compiler_params=_cparams(1),
        name="sb_sample",
    )(page_table.reshape(-1), bias, proj, proj, proj, *([pages(cache_k)] * pps), *([pages(cache_v)] * pps), u, ug)


CONV_HALO = 32


def _conv_body(*refs, g, r, n, has_buf):
    if has_buf:
        ca, cb, buf, w_ref, b_ref, lg_ref, lb_ref, o_ref, new_ref, xp = refs
    else:
        ca, cb, w_ref, b_ref, lg_ref, lb_ref, o_ref, new_ref, xp = refs
    i = pl.program_id(1)
    nh = CONV_W - 1
    ch = ca.shape[-1]

    @pl.when(i == 0)
    def _():
        xp[:, 0:CONV_HALO, :] = jnp.zeros((g, CONV_HALO, ch), F32)
        if has_buf:
            xp[:, CONV_HALO - nh:CONV_HALO, :] = buf[...]

    glu = ca[...] * jax.nn.sigmoid(cb[...])
    xp[:, CONV_HALO:CONV_HALO + r, :] = glu.reshape(g, r, ch)
    base = CONV_HALO - nh
    y = jnp.zeros((g, r, ch), F32) + b_ref[...]
    for j in range(CONV_W):
        y = y + w_ref[j:j + 1, :] * xp[:, base + j:base + j + r, :]
    mu = jnp.mean(y, axis=-1, keepdims=True)
    yc = y - mu
    yn = yc * lax.rsqrt(jnp.mean(yc * yc, axis=-1, keepdims=True) + EPS) * lg_ref[...] + lb_ref[...]
    o_ref[...] = _silu(yn).reshape(g * r, ch).astype(o_ref.dtype)
    tail = xp[:, base + r:CONV_HALO + r, :]
    xp[:, base:CONV_HALO, :] = tail

    @pl.when(i == n - 1)
    def _():
        new_ref[...] = tail


def _conv_branch(proj, off, b, l, buf, layer, w, bias, ln_g, ln_b, tm=512):
    ch = w.shape[1]
    g, r, nt, _ = _row_tiling(b, l, tm)
    n = nt if b == 1 else 1
    ja, jb = off["ca"] // ch, off["cb"] // ch
    rowblk = (lambda bi, i: i) if b == 1 else (lambda bi, i: bi)
    grid = (1, nt) if b == 1 else (nt, 1)
    in_specs = [pl.BlockSpec((g * r, ch), lambda bi, i: (rowblk(bi, i), ja)),
                pl.BlockSpec((g * r, ch), lambda bi, i: (rowblk(bi, i), jb))]
    args = [proj, proj]
    if buf is not None:
        in_specs.append(pl.BlockSpec((None, g, CONV_W - 1, ch), lambda bi, i: (layer, bi, 0, 0)))
        args.append(buf)
    const2 = lambda a: pl.BlockSpec(a.shape, lambda bi, i: (0, 0))
    small = [w, bias.reshape(1, ch), ln_g.reshape(1, ch), ln_b.reshape(1, ch)]
    in_specs += [const2(a) for a in small]
    args += small
    return pl.pallas_call(
        functools.partial(_conv_body, g=g, r=r, n=n, has_buf=buf is not None),
        grid=grid,
        in_specs=in_specs,
        out_specs=[pl.BlockSpec((g * r, ch), lambda bi, i: (rowblk(bi, i), 0)),
                   pl.BlockSpec((g, CONV_W - 1, ch), lambda bi, i: (bi, 0, 0))],
        out_shape=[jax.ShapeDtypeStruct((b * l, ch), BF16),
                   jax.ShapeDtypeStruct((b, CONV_W - 1, ch), F32)],
        scratch_shapes=[pltpu.VMEM((g, CONV_HALO + r, ch), F32)],
        compiler_params=_cparams(2),
        name="conv_branch",
    )(*args)


def _gla_consts():
    kd = N_HEADS * GLA_DK
    sel = (np.arange(kd)[:, None] // GLA_DK) == (np.arange(N_HEADS * HEAD_DIM)[None, :] // HEAD_DIM)
    tri = np.arange(GLA_SUB)[:, None] >= np.arange(GLA_SUB)[None, :]
    return jnp.asarray(sel, BF16), jnp.asarray(tri, BF16)


def _gla_local(q, k, v, glr, wlr, blr, sel, tri, pad):
    c = GLA_SUB
    kd = N_HEADS * GLA_DK
    x = _dot(glr.astype(BF16), wlr) + blr
    la = -_softplus(-x) * (1.0 / GLA_TAU)
    row = lax.broadcasted_iota(jnp.int32, (c, kd), 0)
    if pad:
        la = jnp.where(row >= pad, la, 0.0)
    hi, lo = _split_bf16(la)
    bc = _dot(tri, hi) + _dot(tri, lo)
    qs = q * (GLA_DK ** -0.5)
    blocks = []
    for s in range(pad, c):
        e = jnp.where(row >= s, jnp.exp(jnp.minimum(bc - bc[s:s + 1, :], 0.0)), 0.0)
        blocks.append((qs * k[s:s + 1, :] * e).astype(BF16))
    attb = _dot(jnp.concatenate(blocks, axis=0), sel)
    od = jnp.zeros((c, N_HEADS * HEAD_DIM), F32)
    for n_, s in enumerate(range(pad, c)):
        od = od + attb[n_ * c:(n_ + 1) * c, :] * v[s:s + 1, :]
    blast = bc[c - 1:c, :]
    qe = qs * jnp.exp(bc)
    ke = k * jnp.exp(blast - bc)
    lane_head = lax.broadcasted_iota(jnp.int32, (c, kd), 1) // GLA_DK
    q_st = jnp.concatenate([jnp.where(lane_head == h, qe, 0.0) for h in range(N_HEADS)], axis=0)
    k_st = jnp.concatenate([jnp.where(lane_head == h, ke, 0.0) for h in range(N_HEADS)], axis=0)
    v_st = jnp.concatenate([v[:, HEAD_DIM * h:HEAD_DIM * (h + 1)] for h in range(N_HEADS)], axis=0)
    return od, q_st.astype(BF16), k_st.astype(BF16), v_st.astype(BF16), jnp.exp(blast)


def _gla_carried(local, gt, st):
    od, q_st, k_st, v_st, decay = local
    c = GLA_SUB
    oi = _dot_nt(q_st, st.astype(BF16))
    st_new = decay * st + _dot_tn(v_st, k_st)
    o = od + jnp.concatenate([oi[c * h:c * (h + 1), :] for h in range(N_HEADS)], axis=1)
    outs = []
    for h in range(N_HEADS):
        oh = o[:, HEAD_DIM * h:HEAD_DIM * (h + 1)]
        outs.append(oh * lax.rsqrt(jnp.mean(oh * oh, axis=-1, keepdims=True) + EPS))
    return jnp.concatenate(outs, axis=1) * _silu(gt), st_new


def _gla_body(*refs, rb, n, has_s0):
    if has_s0:
        gq, gk, gv, gg, glr, wlr, blr, sel, tri, s0, o_ref, sn_ref, st = refs
    else:
        gq, gk, gv, gg, glr, wlr, blr, sel, tri, o_ref, sn_ref, st = refs
    i = pl.program_id(1)

    @pl.when(i == 0)
    def _():
        if has_s0:
            st[...] = s0[0]
        else:
            st[...] = jnp.zeros(st.shape, F32)

    c = GLA_SUB
    if rb < c:
        pad = c - rb
        zp = lambda a: jnp.concatenate([jnp.zeros((pad, a.shape[1]), F32), a], axis=0)
        local = _gla_local(zp(gq[...]), zp(gk[...]), zp(gv[...]), zp(glr[...]),
                           wlr[...], blr[...], sel[...], tri[...], pad)
        o, st_new = _gla_carried(local, zp(gg[...]), st[...])
        o_ref[...] = o[pad:, :].astype(o_ref.dtype)
        st[...] = st_new
    else:
        st_run = st[...]
        for m in range(rb // c):
            rs = slice(m * c, (m + 1) * c)
            local = _gla_local(gq[rs, :], gk[rs, :], gv[rs, :], glr[rs, :],
                               wlr[...], blr[...], sel[...], tri[...], 0)
            o, st_run = _gla_carried(local, gg[rs, :], st_run)
            o_ref[rs, :] = o.astype(o_ref.dtype)
        st[...] = st_run

    @pl.when(i == n - 1)
    def _():
        sn_ref[0] = st[...]


def _gla(proj, off, b, l, wlr, blr, s0t, layer, rb=128):
    rb = min(rb, l)
    n = l // rb
    kd = N_HEADS * GLA_DK
    bw = N_HEADS * HEAD_DIM
    sel, tri = _gla_consts()

    def pspec(name, width):
        j = off[name] // width
        return pl.BlockSpec((rb, width), lambda bi, i: (bi * n + i, j))

    const2 = lambda a: pl.BlockSpec(a.shape, lambda bi, i: (0, 0))
    in_specs = [pspec("gq", kd), pspec("gk", kd), pspec("gv", bw), pspec("gg", bw), pspec("glr", LANE),
                const2(wlr), const2(blr), const2(sel), const2(tri)]
    args = [proj, proj, proj, proj, proj, wlr, blr, sel, tri]
    if s0t is not None:
        in_specs.append(pl.BlockSpec((None, 1, HEAD_DIM, kd), lambda bi, i: (layer, bi, 0, 0)))
        args.append(s0t)
    return pl.pallas_call(
        functools.partial(_gla_body, rb=rb, n=n, has_s0=s0t is not None),
        grid=(b, n),
        in_specs=in_specs,
        out_specs=[pl.BlockSpec((rb, bw), lambda bi, i: (bi * n + i, 0)),
                   pl.BlockSpec((1, HEAD_DIM, kd), lambda bi, i: (bi, 0, 0))],
        out_shape=[jax.ShapeDtypeStruct((b * l, bw), BF16),
                   jax.ShapeDtypeStruct((b, HEAD_DIM, kd), F32)],
        scratch_shapes=[pltpu.VMEM((HEAD_DIM, kd), F32)],
        compiler_params=_cparams(2),
        name="gla",
    )(*args)


def _merge_body(a_ref, b_ref, c_ref, d_ref, g0, g1, g2, g3, w_ref, o_ref):
    acc = None
    bw = a_ref.shape[1]
    for i, (br, gl) in enumerate(((a_ref, g0), (b_ref, g1), (c_ref, g2), (d_ref, g3))):
        t = jax.nn.sigmoid(gl[...]) * _dot(br[...], w_ref[bw * i:bw * (i + 1), :])
        acc = t if acc is None else acc + t
    o_ref[...] = acc.astype(o_ref.dtype)


def _merge(branches, proj, wb, layer, tm=512, tn=1024):
    rows, bw = branches[0].shape
    d = wb.shape[2]
    tm = min(tm, rows)
    nj = d // tn
    br_spec = pl.BlockSpec((tm, bw), lambda i, j: (i, 0))
    gate_specs = [pl.BlockSpec((tm, tn), (lambda i, j, q=q: (i, q * nj + j))) for q in range(4)]
    return pl.pallas_call(
        _merge_body,
        grid=(rows // tm, nj),
        in_specs=[br_spec] * 4 + gate_specs + [pl.BlockSpec((None, 4 * bw, tn), lambda i, j: (layer, 0, j))],
        out_specs=pl.BlockSpec((tm, tn), lambda i, j: (i, j)),
        out_shape=jax.ShapeDtypeStruct((rows, d), BF16),
        compiler_params=_cparams(2),
        name="merge",
    )(*branches, proj, proj, proj, proj, wb)


def _resid_body(a_ref, w_ref, x_ref, g_ref, o_ref):
    y = _dot(a_ref[...], w_ref[...])
    o_ref[...] = x_ref[...] + g_ref[...] * y.reshape(o_ref.shape)


def _resid_proj(a, w, layer, x, gate, tm=1024, tn=512):
    b, l, d = x.shape
    k = a.shape[1]
    g, r, nt, xmap = _row_tiling(b, l, tm)
    gmap = (lambda i: (0, 0)) if b == 1 else (lambda i: (i, 0))
    return pl.pallas_call(
        _resid_body,
        grid=(nt, d // tn),
        in_specs=[pl.BlockSpec((g * r, k), lambda i, j: (i, 0)),
                  pl.BlockSpec((None, k, tn), lambda i, j: (layer, 0, j)),
                  pl.BlockSpec((g, r, tn), lambda i, j: xmap(i)[:2] + (j,)),
                  pl.BlockSpec((g, 1, tn), lambda i, j: gmap(i) + (j,))],
        out_specs=pl.BlockSpec((g, r, tn), lambda i, j: xmap(i)[:2] + (j,)),
        out_shape=jax.ShapeDtypeStruct((b, l, d), F32),
        compiler_params=_cparams(2),
        name="resid_proj",
    )(a, w, x, gate)


FFN_HALO = 8


def _ffn_body(*refs, g, r, has_buf):
    if has_buf:
        (x_ref, n_ref, sh_ref, sc_ref, wa_ref, wb_ref, cw_ref, cb_ref, buf_ref,
         y_ref, new_ref, h_ref, ap, carry) = refs
    else:
        (x_ref, n_ref, sh_ref, sc_ref, wa_ref, wb_ref, cw_ref, cb_ref,
         y_ref, new_ref, h_ref, ap, carry) = refs
    i = pl.program_id(0)
    j = pl.program_id(1)
    nh = FFN_CONV_W - 1
    tn = wa_ref.shape[1]

    @pl.when(j == 0)
    def _():
        h = _modulate(x_ref[...], n_ref[...], sh_ref[...], sc_ref[...])
        h_ref[...] = h.reshape(h_ref.shape).astype(BF16)

    hb = h_ref[...]
    a3 = _dot(hb, wa_ref[...]).reshape(g, r, tn)
    b3 = _dot(hb, wb_ref[...]).reshape(g, r, tn)
    if has_buf:
        halo = buf_ref[...]
    else:
        halo = jnp.where(i == 0, 0.0, carry[j])
    ap[:, FFN_HALO - nh:FFN_HALO, :] = halo
    ap[:, FFN_HALO:FFN_HALO + r, :] = a3
    conv = (cb_ref[...] + cw_ref[0:1, :] * ap[:, FFN_HALO - 2:FFN_HALO - 2 + r, :]
            + cw_ref[1:2, :] * ap[:, FFN_HALO - 1:FFN_HALO - 1 + r, :] + cw_ref[2:3, :] * a3)
    y_ref[...] = (_silu(conv) * b3).reshape(g * r, tn).astype(y_ref.dtype)
    tail = a3[:, r - nh:r, :]
    new_ref[...] = tail
    if not has_buf:
        carry[j] = tail


def _ffn_up(x, ng, sh, sc, w_up, layer, cw, cb, buf, tm=1024, tn=512):
    b, l, d = x.shape
    f = w_up.shape[2] // 2
    g, r, nt, xmap = _row_tiling(b, l, tm)
    gmap = (lambda i: (0, 0, 0)) if b == 1 else (lambda i: (i, 0, 0))
    nh = FFN_CONV_W - 1
    nj = f // tn
    in_specs = [pl.BlockSpec((g, r, d), lambda i, j: xmap(i)),
                pl.BlockSpec((1, d), lambda i, j: (0, 0)),
                pl.BlockSpec((g, 1, d), lambda i, j: gmap(i)),
                pl.BlockSpec((g, 1, d), lambda i, j: gmap(i)),
                pl.BlockSpec((None, d, tn), lambda i, j: (layer, 0, j)),
                pl.BlockSpec((None, d, tn), lambda i, j: (layer, 0, nj + j)),
                pl.BlockSpec((FFN_CONV_W, tn), lambda i, j: (0, j)),
                pl.BlockSpec((1, tn), lambda i, j: (0, j))]
    args = [x, ng.reshape(1, d), sh, sc, w_up, w_up, cw, cb.reshape(1, f)]
    if buf is not None:
        in_specs.append(pl.BlockSpec((None, g, nh, tn), lambda i, j: (layer, i, 0, j)))
        args.append(buf)
    n_slots = nt if b == 1 else 1
    newmap = (lambda i, j: (i, 0, 0, j)) if b == 1 else (lambda i, j: (0, i, 0, j))
    return pl.pallas_call(
        functools.partial(_ffn_body, g=g, r=r, has_buf=buf is not None),
        grid=(nt, nj),
        in_specs=in_specs,
        out_specs=[pl.BlockSpec((g * r, tn), lambda i, j: (i, j)),
                   pl.BlockSpec((None, g, nh, tn), newmap)],
        out_shape=[jax.ShapeDtypeStruct((b * l, f), BF16),
                   jax.ShapeDtypeStruct((n_slots, b, nh, f), F32)],
        scratch_shapes=[pltpu.VMEM((g * r, d), BF16),
                        pltpu.VMEM((g, FFN_HALO + r, tn), F32),
                        pltpu.VMEM((nj, g, nh, tn), F32)],
        compiler_params=_cparams(2),
        name="ffn_up",
    )(*args)


def _final_body(x_ref, g_ref, o_ref):
    x = x_ref[...]
    o_ref[...] = x * lax.rsqrt(jnp.mean(x * x, axis=-1, keepdims=True) + EPS) * g_ref[...]


def _final_norm(x, g, tm=512):
    b, l, d = x.shape
    gg, r, nt, xmap = _row_tiling(b, l, tm)
    return pl.pallas_call(
        _final_body,
        grid=(nt,),
        in_specs=[pl.BlockSpec((gg, r, d), xmap), pl.BlockSpec((1, d), lambda i: (0, 0))],
        out_specs=pl.BlockSpec((gg, r, d), xmap),
        out_shape=jax.ShapeDtypeStruct((b, l, d), F32),
        compiler_params=_cparams(1),
        name="final_norm",
    )(x, g.reshape(1, d))


def _cast_body(x_ref, o_ref):
    o_ref[...] = x_ref[...].astype(o_ref.dtype)


def _cast_stack(w, tr=512):
    depth, r, c = w.shape
    spec = pl.BlockSpec((1, tr, c), lambda l, i: (l, i, 0))
    return pl.pallas_call(
        _cast_body, grid=(depth, r // tr), in_specs=[spec], out_specs=spec,
        out_shape=jax.ShapeDtypeStruct(w.shape, BF16), compiler_params=_cparams(2), name="cast_w",
    )(w)


def _prep_up_body(x_ref, o_ref):
    f = x_ref.shape[2]
    o_ref[0, :, :f] = x_ref[0].astype(BF16)
    o_ref[0, :, f:] = jnp.zeros((o_ref.shape[1], o_ref.shape[2] - f), BF16)


def _prep_w_up(w_up, f_pad, tr=256):
    depth, d, n = w_up.shape
    d_ff = n // 2
    return pl.pallas_call(
        _prep_up_body, grid=(depth, d // tr, 2),
        in_specs=[pl.BlockSpec((1, tr, d_ff), lambda l, i, h: (l, i, h))],
        out_specs=pl.BlockSpec((1, tr, f_pad), lambda l, i, h: (l, i, h)),
        out_shape=jax.ShapeDtypeStruct((depth, d, 2 * f_pad), BF16),
        compiler_params=_cparams(3), name="prep_w_up",
    )(w_up)


def _prep_down_body(x_ref, o_ref):
    f = x_ref.shape[1]
    o_ref[0, :f, :] = x_ref[0].astype(BF16)
    o_ref[0, f:, :] = jnp.zeros((o_ref.shape[1] - f, o_ref.shape[2]), BF16)


def _prep_w_down(w_down, f_pad, tc=256):
    depth, d_ff, d = w_down.shape
    return pl.pallas_call(
        _prep_down_body, grid=(depth, d // tc),
        in_specs=[pl.BlockSpec((1, d_ff, tc), lambda l, j: (l, 0, j))],
        out_specs=pl.BlockSpec((1, f_pad, tc), lambda l, j: (l, 0, j)),
        out_shape=jax.ShapeDtypeStruct((depth, f_pad, d), BF16),
        compiler_params=_cparams(2), name="prep_w_down",
    )(w_down)


PREP_TR = 1024


def _prep_in_body(main_ref, extra_ref, o_ref, *, n_gate, n_core):
    j = pl.program_id(1)

    @pl.when(j < n_gate)
    def _():
        o_ref[0] = jnp.concatenate([main_ref[0, GLA_RANK:, :], extra_ref[0]], axis=0).astype(BF16)

    @pl.when((j >= n_gate) & (j < n_gate + n_core))
    def _():
        o_ref[0] = main_ref[0].astype(BF16)

    @pl.when(j == n_gate + n_core)
    def _():
        row = lax.broadcasted_iota(jnp.int32, main_ref.shape[1:], 0)
        o_ref[0] = jnp.where(row < GLA_RANK, main_ref[0], 0.0).astype(BF16)


def _prep_w_in(w_in_t, off):
    depth, n_in, d = w_in_t.shape
    core = off["glr"] - off["rq"]
    assert core % PREP_TR == 0 and (4 * d) % PREP_TR == 0 and n_in == core + GLA_RANK + 4 * d
    n_gate, n_core = 4 * d // PREP_TR, core // PREP_TR
    c0 = core // PREP_TR

    def main_map(l, j):
        return (l, jnp.where(j < n_gate, c0 + j, jnp.where(j < n_gate + n_core, j - n_gate, c0)), 0)

    def extra_map(l, j):
        return (l, jnp.where(j < n_gate, (c0 + j + 1) * (PREP_TR // GLA_RANK), 0), 0)

    return pl.pallas_call(
        functools.partial(_prep_in_body, n_gate=n_gate, n_core=n_core),
        grid=(depth, n_gate + n_core + 1),
        in_specs=[pl.BlockSpec((1, PREP_TR, d), main_map), pl.BlockSpec((1, GLA_RANK, d), extra_map)],
        out_specs=pl.BlockSpec((1, PREP_TR, d), lambda l, j: (l, j, 0)),
        out_shape=jax.ShapeDtypeStruct((depth, off["total"], d), BF16),
        compiler_params=_cparams(2), name="prep_w_in",
    )(w_in_t, w_in_t)


def _pad_cols(a, n):
    return jnp.pad(a, [(0, 0)] * (a.ndim - 1) + [(0, n - a.shape[-1])])


def _layer(x, mods, cos, sin, layer, states, sb_cache, sw, lw, off):
    b, l, d = x.shape
    sh1, sc1, g1, sh2, sc2, g2 = mods
    ret_s0, conv_buf, gla_s0t, ffn_buf = states if states is not None else (None,) * 4
    proj = _modproj(x, lw["norm1_g"], sh1, sc1, sw["w_in"], layer, tm=1024, tn=768)

    o_a, ret_new = _retention(proj, off, b, l, cos, sin, ret_s0, layer)
    if sb_cache is None:
        o_b = _sb_prompt(proj, off, l, lw["sb_bias"])
    else:
        cache_k, cache_v, page_table = sb_cache
        o_b = _sb_sample(proj, off, b, l, lw["sb_bias"], cache_k, cache_v, layer, page_table)
    o_c, conv_new = _conv_branch(proj, off, b, l, conv_buf, layer, lw["conv_w"], lw["conv_b"],
                                 lw["conv_ln_g"], lw["conv_ln_b"])
    o_d, gla_t = _gla(proj, off, b, l, lw["gla_w_lr"], lw["gla_b_lr"], gla_s0t, layer)
    gla_new = gla_t.reshape(b, HEAD_DIM, N_HEADS, GLA_DK).transpose(0, 2, 3, 1)

    merged = _merge((o_a, o_b, o_c, o_d), proj, sw["w_branch"], layer)
    x1 = _resid_proj(merged, sw["w_out"], layer, x, g1)

    y, ffn_slots = _ffn_up(x1, lw["norm2_g"], sh2, sc2, sw["w_up"], layer,
                           lw["ffn_conv_w"], lw["ffn_conv_b"], ffn_buf)
    x2 = _resid_proj(y, sw["w_down"], layer, x1, g2)

    bw = d // 4
    sk = proj[:, off["sk"]:off["sk"] + bw].reshape(b, l, N_HEADS, HEAD_DIM)
    sv = proj[:, off["sv"]:off["sv"] + bw].reshape(b, l, N_HEADS, HEAD_DIM)
    return x2, (sk, sv, ret_new, conv_new, gla_new, ffn_slots[-1])


def kernel(x_prompt, x_sample, cache_sb_k, cache_sb_v, page_table, state_ret, state_conv, state_gla,
           state_ffn_conv, c_prompt, c_sample, norm1_g, norm2_g, w_ada, b_ada, w_in, gla_w_lr, gla_b_lr,
           sb_bias, conv_w, conv_b, conv_ln_g, conv_ln_b, w_branch, w_out, w_up, ffn_conv_w, ffn_conv_b,
           w_down, final_g):
    bp, lp, d = x_prompt.shape
    bs, ls, _ = x_sample.shape
    depth = w_in.shape[0]
    d_ff = w_down.shape[1]
    f_pad = -(-d_ff // 512) * 512
    past_len = page_table.shape[1] * PAGE
    off = _layout(d)

    n_c = bp + bs
    m_pad = -(-n_c // 8) * 8
    c_all = jnp.concatenate([c_prompt, c_sample, jnp.zeros((m_pad - n_c, d), F32)], axis=0)
    mod = _ada(c_all, w_ada, b_ada)

    cos_p, sin_p = _rope_tables(jnp.arange(lp))
    cos_s, sin_s = _rope_tables(past_len + jnp.arange(ls))

    sw = {
        "w_in": _prep_w_in(jnp.swapaxes(w_in, 1, 2), off),
        "w_branch": _cast_stack(w_branch.reshape(depth, -1, d)),
        "w_out": _cast_stack(w_out),
        "w_up": _prep_w_up(w_up, f_pad),
        "w_down": _prep_w_down(w_down, f_pad),
    }
    gla_t = state_gla.transpose(0, 1, 4, 2, 3).reshape(depth, bs, HEAD_DIM, N_HEADS * GLA_DK)
    states_s = (state_ret, state_conv, gla_t, _pad_cols(state_ffn_conv, f_pad))
    sb_cache = (cache_sb_k, cache_sb_v, page_table)

    xp, xs = x_prompt, x_sample
    out_p = [[] for _ in range(6)]
    out_s = [[] for _ in range(6)]
    for l in range(depth):
        lw = {
            "norm1_g": norm1_g[l], "norm2_g": norm2_g[l],
            "gla_w_lr": jnp.pad(gla_w_lr[l], ((0, LANE - GLA_RANK), (0, 0))).astype(BF16),
            "gla_b_lr": gla_b_lr[l].reshape(1, -1),
            "sb_bias": sb_bias[l],
            "conv_w": conv_w[l], "conv_b": conv_b[l], "conv_ln_g": conv_ln_g[l], "conv_ln_b": conv_ln_b[l],
            "ffn_conv_w": _pad_cols(ffn_conv_w[l], f_pad), "ffn_conv_b": _pad_cols(ffn_conv_b[l], f_pad),
        }
        mod_l = mod[l]
        mods_p = tuple(mod_l[:bp, i * d:(i + 1) * d].reshape(bp, 1, d) for i in range(N_ADA))
        mods_s = tuple(mod_l[bp:n_c, i * d:(i + 1) * d].reshape(bs, 1, d) for i in range(N_ADA))

        xp, st_p = _layer(xp, mods_p, cos_p, sin_p, l, None, None, sw, lw, off)
        xs, st_s = _layer(xs, mods_s, cos_s, sin_s, l, states_s, sb_cache, sw, lw, off)
        for i in range(6):
            out_p[i].append(st_p[i])
            out_s[i].append(st_s[i])

    y_prompt = _final_norm(xp, final_g)
    y_sample = _final_norm(xs, final_g)
    kp, vp, rp, cp, gp, fp = [jnp.stack(t, axis=0) for t in out_p]
    ks_, vs_, rs_, cs_, gs_, fs_ = [jnp.stack(t, axis=0) for t in out_s]
    fp = fp[..., :d_ff]
    fs_ = fs_[..., :d_ff]
    return (y_prompt, y_sample, kp, vp, rp, cp, gp, fp, ks_, vs_, rs_, cs_, gs_, fs_)
```

```python
import functools

import numpy as np
import jax
import jax.numpy as jnp
from jax import lax
from jax.experimental import pallas as pl
from jax.experimental.pallas import tpu as pltpu

F32 = jnp.float32
BF16 = jnp.bfloat16

EPS = 1e-6
N_HEADS = 4
HEAD_DIM = 128
GLA_DK = 64
GLA_RANK = 16
GLA_TAU = 16.0
GLA_SUB = 16
ROPE_BASE = 10000.0
PAGE = 128
CONV_W = 31
FFN_CONV_W = 3
N_ADA = 6
LANE = 128
VMEM_LIMIT = 56 * 1024 * 1024

C_GATE = 0


def _layout(d):
    bw = d // 4
    off = {}
    o = 4 * d
    for name, w in (("rq", bw), ("rk", bw), ("rv", bw), ("rg", bw),
                    ("sq", bw), ("sk", bw), ("sv", bw),
                    ("ca", bw), ("cb", bw),
                    ("gq", bw // 2), ("gk", bw // 2), ("gv", bw), ("gg", bw),
                    ("glr", 2 * LANE)):
        off[name] = o
        o += w
    off["total"] = o
    return off


def _cparams(n_axes):
    return pltpu.CompilerParams(dimension_semantics=("arbitrary",) * n_axes,
                                vmem_limit_bytes=VMEM_LIMIT)


def _silu(x):
    return x * jax.nn.sigmoid(x)


LOG2E = 1.4426950408889634


def _softplus(x):
    return jnp.maximum(x, 0.0) + jnp.log(1.0 + jnp.exp2(jnp.abs(x) * (-LOG2E)))


def _split_bf16(x):
    hi = x.astype(BF16)
    lo = (x - hi.astype(F32)).astype(BF16)
    return hi, lo


def _dot(a, b):
    return jnp.dot(a, b, preferred_element_type=F32)


def _dot_nt(a, b):
    return lax.dot_general(a, b, (((1,), (1,)), ((), ())), preferred_element_type=F32)


def _dot_tn(a, b):
    return lax.dot_general(a, b, (((0,), (0,)), ((), ())), preferred_element_type=F32)


def _ada_body(c_ref, w_ref, b_ref, o_ref):
    s = _silu(c_ref[...]).astype(BF16)
    o_ref[0] = _dot(s, w_ref[0].astype(BF16)) + b_ref[0]


def _ada(c_all, w_ada, b_ada):
    depth, d, n = w_ada.shape
    m = c_all.shape[0]
    tn = 1024
    return pl.pallas_call(
        _ada_body,
        grid=(depth, n // tn),
        in_specs=[pl.BlockSpec((m, d), lambda l, j: (0, 0)),
                  pl.BlockSpec((1, d, tn), lambda l, j: (l, 0, j)),
                  pl.BlockSpec((1, 1, tn), lambda l, j: (l, 0, j))],
        out_specs=pl.BlockSpec((1, m, tn), lambda l, j: (l, 0, j)),
        out_shape=jax.ShapeDtypeStruct((depth, m, n), F32),
        compiler_params=_cparams(2),
        name="ada_mod",
    )(c_all, w_ada, b_ada.reshape(depth, 1, n))


def _modulate(x, g, sh, sc):
    ms = jnp.mean(x * x, axis=-1, keepdims=True)
    y = x * lax.rsqrt(ms + EPS) * g
    return y * (1.0 + sc) + sh


def _modproj_body(x_ref, g_ref, sh_ref, sc_ref, w_ref, o_ref, h_ref):
    @pl.when(pl.program_id(1) == 0)
    def _():
        h = _modulate(x_ref[...], g_ref[...], sh_ref[...], sc_ref[...])
        h_ref[...] = h.reshape(h_ref.shape).astype(BF16)

    o_ref[...] = _dot_nt(h_ref[...], w_ref[...])


def _row_tiling(b, l, tm):
    if b == 1:
        r = min(tm, l)
        return 1, r, l // r, (lambda i: (0, i, 0))
    g = min(max(tm // l, 1), b)
    return g, l, b // g, (lambda i: (i, 0, 0))


def _modproj(x, g, sh, sc, w, layer, tm, tn):
    b, l, d = x.shape
    n = w.shape[1]
    gg, r, nt, xmap = _row_tiling(b, l, tm)
    gmap = (lambda i: (0, 0, 0)) if b == 1 else (lambda i: (i, 0, 0))
    return pl.pallas_call(
        _modproj_body,
        grid=(nt, n // tn),
        in_specs=[pl.BlockSpec((gg, r, d), lambda i, j: xmap(i)),
                  pl.BlockSpec((1, d), lambda i, j: (0, 0)),
                  pl.BlockSpec((gg, 1, d), lambda i, j: gmap(i)),
                  pl.BlockSpec((gg, 1, d), lambda i, j: gmap(i)),
                  pl.BlockSpec((None, tn, d), lambda i, j: (layer, j, 0))],
        out_specs=pl.BlockSpec((gg * r, tn), lambda i, j: (i, j)),
        out_shape=jax.ShapeDtypeStruct((b * l, n), F32),
        scratch_shapes=[pltpu.VMEM((gg * r, d), BF16)],
        compiler_params=_cparams(2),
        name="modproj",
    )(x, g.reshape(1, d), sh, sc, w)


def _ret_body(*refs, c, cs, n, has_s0, cdec):
    if has_s0:
        (rq, rk, rv, rg, cos, sin, dm, qd, kd, s0, o_ref, sn_ref, s_scr, kp, vp) = refs
    else:
        (rq, rk, rv, rg, cos, sin, dm, qd, kd, o_ref, sn_ref, s_scr, kp, vp) = refs
        s0 = None
    i = pl.program_id(1)

    @pl.when(i == 0)
    def _():
        if has_s0:
            s_scr[...] = s0[0]
        else:
            s_scr[...] = jnp.zeros(s_scr.shape, F32)
        if c < LANE:
            kp[...] = jnp.zeros(kp.shape, F32)
            vp[...] = jnp.zeros(vp.shape, F32)

    for u in range(cs):
        _ret_chunk(slice(u * c, (u + 1) * c), rq, rk, rv, rg, cos, sin, dm, qd, kd, o_ref, s_scr, kp, vp,
                   c, cdec)

    @pl.when(i == n - 1)
    def _():
        sn_ref[0] = s_scr[...]


def _ret_chunk(rows, rq, rk, rv, rg, cos, sin, dm, qd, kd, o_ref, s_scr, kp, vp, c, cdec):
    c2 = cos[rows, :]
    s2 = sin[rows, :]
    heads = [slice(HEAD_DIM * h, HEAD_DIM * (h + 1)) for h in range(N_HEADS)]
    stage = []
    for h, sl in enumerate(heads):
        q = rq[rows, sl]
        k = rk[rows, sl]
        v = rv[rows, sl]
        qr = q * c2 + pltpu.roll(q, HEAD_DIM // 2, 1) * s2
        kr = (k * c2 + pltpu.roll(k, HEAD_DIM // 2, 1) * s2) * (HEAD_DIM ** -0.5)
        if c < LANE:
            kp[0, 0:c, sl] = kr
            kp[1, 0:c, sl] = kr * kd[h]
            vp[0:c, sl] = v
            kb = kp[0, :, sl].astype(BF16)
            kdb = kp[1, :, sl].astype(BF16)
            vb = vp[:, sl].astype(BF16)
        else:
            kb = kr.astype(BF16)
            vb = v.astype(BF16)
            kdb = (kr * kd[h]).astype(BF16)
        att = _dot_nt(qr.astype(BF16), kb) * dm[h]
        sh_ = s_scr[h]
        cross = _dot((qr * qd[h]).astype(BF16), sh_.astype(BF16))
        s_scr[h] = cdec[h] * sh_ + _dot_tn(kdb, vb)
        stage.append((att.astype(BF16), vb, cross))
    for (att, vb, cross), sl in zip(stage, heads):
        o = _dot(att, vb) + cross
        o = o - jnp.mean(o, axis=-1, keepdims=True)
        o = o * lax.rsqrt(jnp.mean(o * o, axis=-1, keepdims=True) + EPS)
        o_ref[rows, sl] = (o * _silu(rg[rows, sl])).astype(o_ref.dtype)


def _rope_tables(pos):
    half = HEAD_DIM // 2
    inv = ROPE_BASE ** (-jnp.arange(half, dtype=F32) / half)
    ang = pos.astype(F32)[:, None] * inv[None, :]
    cos = jnp.cos(ang)
    sin = jnp.sin(ang)
    return jnp.concatenate([cos, cos], axis=-1), jnp.concatenate([-sin, sin], axis=-1)


def _ret_consts(c):
    ck = max(c, LANE)
    lg = np.log1p(-np.exp2(-5.0 - np.arange(N_HEADS, dtype=np.float64)))
    idx = np.arange(c, dtype=np.float64)
    rel = idx[:, None] - idx[None, :]
    dmask = np.where(rel[None] >= 0, np.exp(np.maximum(rel, 0.0)[None] * lg[:, None, None]), 0.0)
    dm = np.zeros((N_HEADS, c, ck))
    dm[:, :, :c] = dmask
    qd = np.exp((idx[None, :] + 1.0) * lg[:, None])[:, :, None] * np.ones((1, 1, HEAD_DIM))
    kd = np.exp((c - 1.0 - idx)[None, :] * lg[:, None])[:, :, None] * np.ones((1, 1, HEAD_DIM))
    cdec = tuple(float(v) for v in np.exp(c * lg))
    return (jnp.asarray(dm, F32), jnp.asarray(qd, F32), jnp.asarray(kd, F32), cdec)


def _retention(proj, off, b, l, cos, sin, s0, layer):
    c = 128 if l % 128 == 0 else l
    cs = 4 if l % (4 * c) == 0 else 1
    n = l // (c * cs)
    ck = max(c, LANE)
    dm, qd, kd, cdec = _ret_consts(c)
    bw = N_HEADS * HEAD_DIM
    cb = lambda name: off[name] // bw

    def pspec(name):
        j = cb(name)
        return pl.BlockSpec((c * cs, bw), lambda bi, i: (bi * n + i, j))

    tab = pl.BlockSpec((c * cs, HEAD_DIM), lambda bi, i: (i, 0))
    const3 = lambda a: pl.BlockSpec(a.shape, lambda bi, i: (0, 0, 0))
    in_specs = [pspec("rq"), pspec("rk"), pspec("rv"), pspec("rg"), tab, tab,
                const3(dm), const3(qd), const3(kd)]
    args = [proj, proj, proj, proj, cos, sin, dm, qd, kd]
    if s0 is not None:
        in_specs.append(pl.BlockSpec((None, 1, N_HEADS, HEAD_DIM, HEAD_DIM),
                                     lambda bi, i: (layer, bi, 0, 0, 0)))
        args.append(s0)
    return pl.pallas_call(
        functools.partial(_ret_body, c=c, cs=cs, n=n, has_s0=s0 is not None, cdec=cdec),
        grid=(b, n),
        in_specs=in_specs,
        out_specs=[pl.BlockSpec((c * cs, bw), lambda bi, i: (bi * n + i, 0)),
                   pl.BlockSpec((1, N_HEADS, HEAD_DIM, HEAD_DIM), lambda bi, i: (bi, 0, 0, 0))],
        out_shape=[jax.ShapeDtypeStruct((b * l, bw), BF16),
                   jax.ShapeDtypeStruct((b, N_HEADS, HEAD_DIM, HEAD_DIM), F32)],
        scratch_shapes=[pltpu.VMEM((N_HEADS, HEAD_DIM, HEAD_DIM), F32),
                        pltpu.VMEM((2, ck, bw), F32), pltpu.VMEM((ck, bw), F32)],
        compiler_params=_cparams(2),
        name="retention",
    )(*args)


def _sb_suffix_matrix(tk):
    j = np.arange(tk)[:, None]
    s = np.arange(tk + LANE)[None, :]
    u = np.where(s < tk, j > s, True)
    return jnp.asarray(u, BF16)


def _sb_scores(q, k, bias, valid, u):
    z = _dot_nt(q, k) + bias
    sp = _softplus(z)
    spm = sp if valid is None else jnp.where(valid, sp, 0.0)
    return z - sp, _dot(spm.astype(BF16), u)


def _sb_values(lz, r, v, valid, carry):
    tk = v.shape[0]
    w = jnp.exp(lz - (r[:, :tk] + jnp.concatenate([carry] * (tk // LANE), axis=1)))
    if valid is not None:
        w = jnp.where(valid, w, 0.0)
    return _dot(w.astype(BF16), v), carry + r[:, tk:]


def _sbp_body(qb_ref, kb_ref, bias_ref, q_ref, k_ref, v_ref, u_ref, o_ref, carry, acc, *, tq, tk, rc):
    h = pl.program_id(0)
    p = pl.program_id(1)
    qb = qb_ref[p]
    kb = kb_ref[p]
    ratio = tq // tk
    first_masked = ratio * qb

    @pl.when(kb == first_masked + ratio - 1)
    def _():
        carry[...] = jnp.zeros(carry.shape, F32)
        acc[...] = jnp.zeros(acc.shape, F32)

    def update(masked):
        kbf = k_ref[...].astype(BF16)
        vbf = v_ref[...].astype(BF16)
        chunks = [slice(c * rc, (c + 1) * rc) for c in range(tq // rc)]

        def valid_of(rs):
            if not masked:
                return None
            rows = qb * tq + rs.start + lax.broadcasted_iota(jnp.int32, (rc, tk), 0)
            cols = kb * tk + lax.broadcasted_iota(jnp.int32, (rc, tk), 1)
            return cols < rows

        scores = [_sb_scores((q_ref[rs, :] * (HEAD_DIM ** -0.5)).astype(BF16), kbf, bias_ref[h],
                             valid_of(rs), u_ref[...]) for rs in chunks]
        for rs, (lz, r) in zip(chunks, scores):
            pv, cnew = _sb_values(lz, r, vbf, valid_of(rs), carry[rs, :])
            acc[rs, :] += pv
            carry[rs, :] = cnew

    @pl.when(kb >= first_masked)
    def _():
        update(True)

    @pl.when(kb < first_masked)
    def _():
        update(False)

    @pl.when(kb == 0)
    def _():
        o_ref[...] = acc[...].astype(o_ref.dtype)


def _sb_prompt(proj, off, l, bias, tq=1024, tk=512, rc=512):
    tq, tk = min(tq, l), min(tk, l)
    rc = min(rc, tq)
    nq = l // tq
    ratio = tq // tk
    qb = np.concatenate([np.full(ratio * (i + 1), i) for i in range(nq)]).astype(np.int32)
    kb = np.concatenate([np.arange(ratio * (i + 1) - 1, -1, -1) for i in range(nq)]).astype(np.int32)
    cq, ck, cv = (off[k] // HEAD_DIM for k in ("sq", "sk", "sv"))
    u = _sb_suffix_matrix(tk)
    grid_spec = pltpu.PrefetchScalarGridSpec(
        num_scalar_prefetch=2,
        grid=(N_HEADS, len(qb)),
        in_specs=[pl.BlockSpec(memory_space=pltpu.SMEM),
                  pl.BlockSpec((tq, HEAD_DIM), lambda h, p, qb, kb: (qb[p], cq + h)),
                  pl.BlockSpec((tk, HEAD_DIM), lambda h, p, qb, kb: (kb[p], ck + h)),
                  pl.BlockSpec((tk, HEAD_DIM), lambda h, p, qb, kb: (kb[p], cv + h)),
                  pl.BlockSpec(u.shape, lambda h, p, qb, kb: (0, 0))],
        out_specs=pl.BlockSpec((tq, HEAD_DIM), lambda h, p, qb, kb: (qb[p], h)),
        scratch_shapes=[pltpu.VMEM((tq, LANE), F32), pltpu.VMEM((tq, HEAD_DIM), F32)])
    return pl.pallas_call(
        functools.partial(_sbp_body, tq=tq, tk=tk, rc=rc),
        grid_spec=grid_spec,
        out_shape=jax.ShapeDtypeStruct((l, N_HEADS * HEAD_DIM), BF16),
        compiler_params=_cparams(2),
        name="sb_prompt",
    )(jnp.asarray(qb), jnp.asarray(kb), bias, proj, proj, proj, u)


def _sbs_suffix_matrix():
    s = np.arange(PAGE + 16)[:, None]
    j = np.arange(PAGE)[None, :]
    return jnp.asarray(np.where(s < PAGE, j > s, True), BF16)


def _sbs_page(k_heads, v_heads, qrows, bias_row, valid, u, carry, l):
    z = None
    for h in range(N_HEADS):
        zh = _dot_nt(k_heads[h], qrows[h])
        z = zh if z is None else z + zh
    z = z * (HEAD_DIM ** -0.5) + bias_row
    sp = _softplus(z)
    spm = sp if valid is None else jnp.where(valid, sp, 0.0)
    r = _dot(u, spm.astype(BF16))
    w = jnp.exp((z - sp) - (r[:PAGE] + carry[0:1, :]))
    if valid is not None:
        w = jnp.where(valid, w, 0.0)
    wt = jnp.transpose(w)
    pv = [_dot(wt[l * h:l * (h + 1), :].astype(BF16), v_heads[h]) for h in range(N_HEADS)]
    return pv, carry + r[PAGE:PAGE + 8]


def _sbs_group_matrix():
    n = PAGE * N_HEADS
    r = np.arange(n + 16)[:, None]
    c = np.arange(n)[None, :]
    return jnp.asarray(np.where(r < n, c // N_HEADS > r // N_HEADS, True), BF16)


def _lane_groups_sum(x, width):
    out = x
    for g in range(1, LANE // width):
        out = out + pltpu.roll(x, g * width, 1)
    return out


def _sbs_group_scores(kpages, qbd, bias_g, match, ug):
    kcat = jnp.concatenate([kp[...].astype(BF16) for kp in kpages], axis=1)
    z = _dot(kcat, qbd) * (HEAD_DIM ** -0.5) + bias_g
    sp = _softplus(z)
    return z - sp, _dot(ug, jnp.where(match, sp, 0.0).astype(BF16))


def _sbs_group_values(lz, r, vpages, match, carry, cp):
    n = PAGE * N_HEADS
    tot = r[n:n + 8]
    lane = lax.broadcasted_iota(jnp.int32, tot.shape, 1)
    newer_pages = jnp.zeros(tot.shape, F32)
    for g in range(1, LANE // cp):
        newer_pages = newer_pages + jnp.where(lane >= g * cp, pltpu.roll(tot, g * cp, 1), 0.0)
    w = jnp.where(match, jnp.exp(lz - (r[:n] + (carry + newer_pages)[0:1, :])), 0.0)
    wt = jnp.transpose(w)
    pv = None
    for p, vp in enumerate(vpages):
        t = _dot(wt[cp * p:cp * (p + 1), :].astype(BF16), vp[...].astype(BF16))
        pv = t if pv is None else pv + t
    return pv, carry + _lane_groups_sum(tot, cp)


def _sbs_body(*refs, l, pps):
    pt_ref, bias_ref, q_ref, kn_ref, vn_ref = refs[:5]
    kpages = refs[5:5 + pps]
    vpages = refs[5 + pps:5 + 2 * pps]
    u_ref, ug_ref, o_ref, qbd, kp, vp = refs[5 + 2 * pps:]
    cp = N_HEADS * l
    group = LANE // cp
    shift = l.bit_length() - 1
    n = PAGE * N_HEADS
    lane = lax.broadcasted_iota(jnp.int32, (1, LANE), 1)
    pair_head = lax.shift_right_logical(lane & (cp - 1), shift)
    bias_g = jnp.zeros((1, LANE), F32)
    for h in range(N_HEADS):
        bias_g = jnp.where(pair_head == h, bias_ref[h], bias_g)

    row_head = lax.shift_right_logical(lax.broadcasted_iota(jnp.int32, (LANE, HEAD_DIM), 0), shift)
    qrows = []
    for h in range(N_HEADS):
        qt = jnp.concatenate([q_ref[:, HEAD_DIM * h:HEAD_DIM * (h + 1)]] * (LANE // l), axis=0)
        qrows.append(jnp.where(row_head == h, qt, 0.0))
    qall_t = _lane_groups_sum(jnp.transpose(sum(qrows[1:], qrows[0])), cp)
    lane_group = lax.shift_right_logical(lax.broadcasted_iota(jnp.int32, (HEAD_DIM, LANE), 1),
                                         cp.bit_length() - 1)
    for p in range(group):
        qbd[HEAD_DIM * p:HEAD_DIM * (p + 1), :] = jnp.where(lane_group == p, qall_t, 0.0).astype(BF16)

    row = lax.broadcasted_iota(jnp.int32, (n, LANE), 0)
    match = (row & (N_HEADS - 1)) == lax.shift_right_logical(
        lax.broadcasted_iota(jnp.int32, (n, LANE), 1) & (cp - 1), shift)
    groups = [slice(gi * group, (gi + 1) * group) for gi in range(pps // group)]
    scores = [_sbs_group_scores(kpages[sl], qbd[...], bias_g, match, ug_ref[...]) for sl in groups]

    kp[...] = jnp.zeros(kp.shape, F32)
    vp[...] = jnp.zeros(vp.shape, F32)
    kp[0:l, :] = kn_ref[...]
    vp[0:l, :] = vn_ref[...]
    key = lax.broadcasted_iota(jnp.int32, (PAGE, LANE), 0)
    cidx = lax.broadcasted_iota(jnp.int32, (PAGE, LANE), 1)
    valid = key < (cidx & (l - 1))
    heads = lambda ref: [ref[:, HEAD_DIM * h:HEAD_DIM * (h + 1)].astype(BF16) for h in range(N_HEADS)]
    pv_new, c_new = _sbs_page(heads(kp), heads(vp), [q.astype(BF16) for q in qrows], bias_g, valid,
                              u_ref[...], jnp.zeros((8, LANE), F32), l)
    a_run = jnp.concatenate(pv_new, axis=0)
    c_run = _lane_groups_sum(jnp.where(lane < cp, c_new, 0.0), cp)
    for sl, (lz, r) in zip(groups, scores):
        pv, c_run = _sbs_group_values(lz, r, vpages[sl], match, c_run, cp)
        a_run = a_run + pv
    for h in range(N_HEADS):
        o_ref[:, HEAD_DIM * h:HEAD_DIM * (h + 1)] = a_run[l * h:l * (h + 1), :].astype(o_ref.dtype)


def _sb_sample(proj, off, b, l, bias, cache_k, cache_v, layer, page_table):
    n_pages = page_table.shape[1]
    cp = N_HEADS * l
    group = LANE // cp
    pps = n_pages
    assert l & (l - 1) == 0 and LANE % cp == 0 and pps % group == 0
    bw = N_HEADS * HEAD_DIM
    u = _sbs_suffix_matrix()
    ug = _sbs_group_matrix()
    cq, ckn, cvn = (off[k] // bw for k in ("sq", "sk", "sv"))
    pages = lambda c: c.reshape(c.shape[0], c.shape[1], PAGE * N_HEADS, HEAD_DIM)

    def page_spec(pp):
        return pl.BlockSpec((None, None, PAGE * N_HEADS, HEAD_DIM),
                            lambda bi, pt: (layer, pt[bi * n_pages + n_pages - 1 - pp], 0, 0))

    in_specs = ([pl.BlockSpec(memory_space=pltpu.SMEM),
                 pl.BlockSpec((l, bw), lambda bi, pt: (bi, cq)),
                 pl.BlockSpec((l, bw), lambda bi, pt: (bi, ckn)),
                 pl.BlockSpec((l, bw), lambda bi, pt: (bi, cvn))]
                + [page_spec(pp) for pp in range(pps)] * 2
                + [pl.BlockSpec(u.shape, lambda bi, pt: (0, 0)),
                   pl.BlockSpec(ug.shape, lambda bi, pt: (0, 0))])
    grid_spec = pltpu.PrefetchScalarGridSpec(
        num_scalar_prefetch=1,
        grid=(b,),
        in_specs=in_specs,
        out_specs=pl.BlockSpec((l, bw), lambda bi, pt: (bi, 0)),
        scratch_shapes=[pltpu.VMEM((group * HEAD_DIM, LANE), BF16),
                        pltpu.VMEM((PAGE, bw), F32), pltpu.VMEM((PAGE, bw), F32)])
    return pl.pallas_call(
        functools.partial(_sbs_body, l=l, pps=pps),
        grid_spec=grid_spec,
        out_shape=jax.ShapeDtypeStruct((b * l, bw), BF16),
        compiler_params=_cparams(1),
        name="sb_sample",
    )(page_table.reshape(-1), bias, proj, proj, proj, *([pages(cache_k)] * pps), *([pages(cache_v)] * pps), u, ug)


CONV_HALO = 32


def _conv_body(*refs, g, r, n, has_buf):
    if has_buf:
        ca, cb, buf, w_ref, b_ref, lg_ref, lb_ref, o_ref, new_ref, xp = refs
    else:
        ca, cb, w_ref, b_ref, lg_ref, lb_ref, o_ref, new_ref, xp = refs
    i = pl.program_id(1)
    nh = CONV_W - 1
    ch = ca.shape[-1]

    @pl.when(i == 0)
    def _():
        xp[:, 0:CONV_HALO, :] = jnp.zeros((g, CONV_HALO, ch), F32)
        if has_buf:
            xp[:, CONV_HALO - nh:CONV_HALO, :] = buf[...]

    glu = ca[...] * jax.nn.sigmoid(cb[...])
    xp[:, CONV_HALO:CONV_HALO + r, :] = glu.reshape(g, r, ch)
    base = CONV_HALO - nh
    y = jnp.zeros((g, r, ch), F32) + b_ref[...]
    for j in range(CONV_W):
        y = y + w_ref[j:j + 1, :] * xp[:, base + j:base + j + r, :]
    mu = jnp.mean(y, axis=-1, keepdims=True)
    yc = y - mu
    yn = yc * lax.rsqrt(jnp.mean(yc * yc, axis=-1, keepdims=True) + EPS) * lg_ref[...] + lb_ref[...]
    o_ref[...] = _silu(yn).reshape(g * r, ch).astype(o_ref.dtype)
    tail = xp[:, base + r:CONV_HALO + r, :]
    xp[:, base:CONV_HALO, :] = tail

    @pl.when(i == n - 1)
    def _():
        new_ref[...] = tail


def _conv_branch(proj, off, b, l, buf, layer, w, bias, ln_g, ln_b, tm=512):
    ch = w.shape[1]
    g, r, nt, _ = _row_tiling(b, l, tm)
    n = nt if b == 1 else 1
    ja, jb = off["ca"] // ch, off["cb"] // ch
    rowblk = (lambda bi, i: i) if b == 1 else (lambda bi, i: bi)
    grid = (1, nt) if b == 1 else (nt, 1)
    in_specs = [pl.BlockSpec((g * r, ch), lambda bi, i: (rowblk(bi, i), ja)),
                pl.BlockSpec((g * r, ch), lambda bi, i: (rowblk(bi, i), jb))]
    args = [proj, proj]
    if buf is not None:
        in_specs.append(pl.BlockSpec((None, g, CONV_W - 1, ch), lambda bi, i: (layer, bi, 0, 0)))
        args.append(buf)
    const2 = lambda a: pl.BlockSpec(a.shape, lambda bi, i: (0, 0))
    small = [w, bias.reshape(1, ch), ln_g.reshape(1, ch), ln_b.reshape(1, ch)]
    in_specs += [const2(a) for a in small]
    args += small
    return pl.pallas_call(
        functools.partial(_conv_body, g=g, r=r, n=n, has_buf=buf is not None),
        grid=grid,
        in_specs=in_specs,
        out_specs=[pl.BlockSpec((g * r, ch), lambda bi, i: (rowblk(bi, i), 0)),
                   pl.BlockSpec((g, CONV_W - 1, ch), lambda bi, i: (bi, 0, 0))],
        out_shape=[jax.ShapeDtypeStruct((b * l, ch), BF16),
                   jax.ShapeDtypeStruct((b, CONV_W - 1, ch), F32)],
        scratch_shapes=[pltpu.VMEM((g, CONV_HALO + r, ch), F32)],
        compiler_params=_cparams(2),
        name="conv_branch",
    )(*args)


def _gla_consts():
    kd = N_HEADS * GLA_DK
    sel = (np.arange(kd)[:, None] // GLA_DK) == (np.arange(N_HEADS * HEAD_DIM)[None, :] // HEAD_DIM)
    tri = np.arange(GLA_SUB)[:, None] >= np.arange(GLA_SUB)[None, :]
    return jnp.asarray(sel, BF16), jnp.asarray(tri, BF16)


def _gla_local(q, k, v, glr, wlr, blr, sel, tri, pad):
    c = GLA_SUB
    kd = N_HEADS * GLA_DK
    x = _dot(glr.astype(BF16), wlr) + blr
    la = -_softplus(-x) * (1.0 / GLA_TAU)
    row = lax.broadcasted_iota(jnp.int32, (c, kd), 0)
    if pad:
        la = jnp.where(row >= pad, la, 0.0)
    hi, lo = _split_bf16(la)
    bc = _dot(tri, hi) + _dot(tri, lo)
    qs = q * (GLA_DK ** -0.5)
    blocks = []
    for s in range(pad, c):
        e = jnp.where(row >= s, jnp.exp(jnp.minimum(bc - bc[s:s + 1, :], 0.0)), 0.0)
        blocks.append((qs * k[s:s + 1, :] * e).astype(BF16))
    attb = _dot(jnp.concatenate(blocks, axis=0), sel)
    od = jnp.zeros((c, N_HEADS * HEAD_DIM), F32)
    for n_, s in enumerate(range(pad, c)):
        od = od + attb[n_ * c:(n_ + 1) * c, :] * v[s:s + 1, :]
    blast = bc[c - 1:c, :]
    qe = qs * jnp.exp(bc)
    ke = k * jnp.exp(blast - bc)
    lane_head = lax.broadcasted_iota(jnp.int32, (c, kd), 1) // GLA_DK
    q_st = jnp.concatenate([jnp.where(lane_head == h, qe, 0.0) for h in range(N_HEADS)], axis=0)
    k_st = jnp.concatenate([jnp.where(lane_head == h, ke, 0.0) for h in range(N_HEADS)], axis=0)
    v_st = jnp.concatenate([v[:, HEAD_DIM * h:HEAD_DIM * (h + 1)] for h in range(N_HEADS)], axis=0)
    return od, q_st.astype(BF16), k_st.astype(BF16), v_st.astype(BF16), jnp.exp(blast)


def _gla_carried(local, gt, st):
    od, q_st, k_st, v_st, decay = local
    c = GLA_SUB
    oi = _dot_nt(q_st, st.astype(BF16))
    st_new = decay * st + _dot_tn(v_st, k_st)
    o = od + jnp.concatenate([oi[c * h:c * (h + 1), :] for h in range(N_HEADS)], axis=1)
    outs = []
    for h in range(N_HEADS):
        oh = o[:, HEAD_DIM * h:HEAD_DIM * (h + 1)]
        outs.append(oh * lax.rsqrt(jnp.mean(oh * oh, axis=-1, keepdims=True) + EPS))
    return jnp.concatenate(outs, axis=1) * _silu(gt), st_new


def _gla_body(*refs, rb, n, has_s0):
    if has_s0:
        gq, gk, gv, gg, glr, wlr, blr, sel, tri, s0, o_ref, sn_ref, st = refs
    else:
        gq, gk, gv, gg, glr, wlr, blr, sel, tri, o_ref, sn_ref, st = refs
    i = pl.program_id(1)

    @pl.when(i == 0)
    def _():
        if has_s0:
            st[...] = s0[0]
        else:
            st[...] = jnp.zeros(st.shape, F32)

    c = GLA_SUB
    if rb < c:
        pad = c - rb
        zp = lambda a: jnp.concatenate([jnp.zeros((pad, a.shape[1]), F32), a], axis=0)
        local = _gla_local(zp(gq[...]), zp(gk[...]), zp(gv[...]), zp(glr[...]),
                           wlr[...], blr[...], sel[...], tri[...], pad)
        o, st_new = _gla_carried(local, zp(gg[...]), st[...])
        o_ref[...] = o[pad:, :].astype(o_ref.dtype)
        st[...] = st_new
    else:
        st_run = st[...]
        for m in range(rb // c):
            rs = slice(m * c, (m + 1) * c)
            local = _gla_local(gq[rs, :], gk[rs, :], gv[rs, :], glr[rs, :],
                               wlr[...], blr[...], sel[...], tri[...], 0)
            o, st_run = _gla_carried(local, gg[rs, :], st_run)
            o_ref[rs, :] = o.astype(o_ref.dtype)
        st[...] = st_run

    @pl.when(i == n - 1)
    def _():
        sn_ref[0] = st[...]


def _gla(proj, off, b, l, wlr, blr, s0t, layer, rb=256):
    rb = min(rb, l)
    n = l // rb
    kd = N_HEADS * GLA_DK
    bw = N_HEADS * HEAD_DIM
    sel, tri = _gla_consts()

    def pspec(name, width):
        j = off[name] // width
        return pl.BlockSpec((rb, width), lambda bi, i: (bi * n + i, j))

    const2 = lambda a: pl.BlockSpec(a.shape, lambda bi, i: (0, 0))
    in_specs = [pspec("gq", kd), pspec("gk", kd), pspec("gv", bw), pspec("gg", bw), pspec("glr", LANE),
                const2(wlr), const2(blr), const2(sel), const2(tri)]
    args = [proj, proj, proj, proj, proj, wlr, blr, sel, tri]
    if s0t is not None:
        in_specs.append(pl.BlockSpec((None, 1, HEAD_DIM, kd), lambda bi, i: (layer, bi, 0, 0)))
        args.append(s0t)
    return pl.pallas_call(
        functools.partial(_gla_body, rb=rb, n=n, has_s0=s0t is not None),
        grid=(b, n),
        in_specs=in_specs,
        out_specs=[pl.BlockSpec((rb, bw), lambda bi, i: (bi * n + i, 0)),
                   pl.BlockSpec((1, HEAD_DIM, kd), lambda bi, i: (bi, 0, 0))],
        out_shape=[jax.ShapeDtypeStruct((b * l, bw), BF16),
                   jax.ShapeDtypeStruct((b, HEAD_DIM, kd), F32)],
        scratch_shapes=[pltpu.VMEM((HEAD_DIM, kd), F32)],
        compiler_params=_cparams(2),
        name="gla",
    )(*args)


def _merge_body(a_ref, b_ref, c_ref, d_ref, g0, g1, g2, g3, w_ref, o_ref):
    acc = None
    bw = a_ref.shape[1]
    for i, (br, gl) in enumerate(((a_ref, g0), (b_ref, g1), (c_ref, g2), (d_ref, g3))):
        t = jax.nn.sigmoid(gl[...]) * _dot(br[...], w_ref[bw * i:bw * (i + 1), :])
        acc = t if acc is None else acc + t
    o_ref[...] = acc.astype(o_ref.dtype)


def _merge(branches, proj, wb, layer, tm=512, tn=1024):
    rows, bw = branches[0].shape
    d = wb.shape[2]
    tm = min(tm, rows)
    nj = d // tn
    br_spec = pl.BlockSpec((tm, bw), lambda i, j: (i, 0))
    gate_specs = [pl.BlockSpec((tm, tn), (lambda i, j, q=q: (i, q * nj + j))) for q in range(4)]
    return pl.pallas_call(
        _merge_body,
        grid=(rows // tm, nj),
        in_specs=[br_spec] * 4 + gate_specs + [pl.BlockSpec((None, 4 * bw, tn), lambda i, j: (layer, 0, j))],
        out_specs=pl.BlockSpec((tm, tn), lambda i, j: (i, j)),
        out_shape=jax.ShapeDtypeStruct((rows, d), BF16),
        compiler_params=_cparams(2),
        name="merge",
    )(*branches, proj, proj, proj, proj, wb)


def _resid_body(a_ref, w_ref, x_ref, g_ref, o_ref):
    y = _dot(a_ref[...], w_ref[...])
    o_ref[...] = x_ref[...] + g_ref[...] * y.reshape(o_ref.shape)


def _resid_proj(a, w, layer, x, gate, tm=1024, tn=512):
    b, l, d = x.shape
    k = a.shape[1]
    g, r, nt, xmap = _row_tiling(b, l, tm)
    gmap = (lambda i: (0, 0)) if b == 1 else (lambda i: (i, 0))
    return pl.pallas_call(
        _resid_body,
        grid=(nt, d // tn),
        in_specs=[pl.BlockSpec((g * r, k), lambda i, j: (i, 0)),
                  pl.BlockSpec((None, k, tn), lambda i, j: (layer, 0, j)),
                  pl.BlockSpec((g, r, tn), lambda i, j: xmap(i)[:2] + (j,)),
                  pl.BlockSpec((g, 1, tn), lambda i, j: gmap(i) + (j,))],
        out_specs=pl.BlockSpec((g, r, tn), lambda i, j: xmap(i)[:2] + (j,)),
        out_shape=jax.ShapeDtypeStruct((b, l, d), F32),
        compiler_params=_cparams(2),
        name="resid_proj",
    )(a, w, x, gate)


FFN_HALO = 8


def _ffn_body(*refs, g, r, has_buf):
    if has_buf:
        (x_ref, n_ref, sh_ref, sc_ref, wa_ref, wb_ref, cw_ref, cb_ref, buf_ref,
         y_ref, new_ref, h_ref, ap, carry) = refs
    else:
        (x_ref, n_ref, sh_ref, sc_ref, wa_ref, wb_ref, cw_ref, cb_ref,
         y_ref, new_ref, h_ref, ap, carry) = refs
    i = pl.program_id(0)
    j = pl.program_id(1)
    nh = FFN_CONV_W - 1
    tn = wa_ref.shape[1]

    @pl.when(j == 0)
    def _():
        h = _modulate(x_ref[...], n_ref[...], sh_ref[...], sc_ref[...])
        h_ref[...] = h.reshape(h_ref.shape).astype(BF16)

    hb = h_ref[...]
    a3 = _dot(hb, wa_ref[...]).reshape(g, r, tn)
    b3 = _dot(hb, wb_ref[...]).reshape(g, r, tn)
    if has_buf:
        halo = buf_ref[...]
    else:
        halo = jnp.where(i == 0, 0.0, carry[j])
    ap[:, FFN_HALO - nh:FFN_HALO, :] = halo
    ap[:, FFN_HALO:FFN_HALO + r, :] = a3
    conv = (cb_ref[...] + cw_ref[0:1, :] * ap[:, FFN_HALO - 2:FFN_HALO - 2 + r, :]
            + cw_ref[1:2, :] * ap[:, FFN_HALO - 1:FFN_HALO - 1 + r, :] + cw_ref[2:3, :] * a3)
    y_ref[...] = (_silu(conv) * b3).reshape(g * r, tn).astype(y_ref.dtype)
    tail = a3[:, r - nh:r, :]
    new_ref[...] = tail
    if not has_buf:
        carry[j] = tail


def _ffn_up(x, ng, sh, sc, w_up, layer, cw, cb, buf, tm=1024, tn=512):
    b, l, d = x.shape
    f = w_up.shape[2] // 2
    g, r, nt, xmap = _row_tiling(b, l, tm)
    gmap = (lambda i: (0, 0, 0)) if b == 1 else (lambda i: (i, 0, 0))
    nh = FFN_CONV_W - 1
    nj = f // tn
    in_specs = [pl.BlockSpec((g, r, d), lambda i, j: xmap(i)),
                pl.BlockSpec((1, d), lambda i, j: (0, 0)),
                pl.BlockSpec((g, 1, d), lambda i, j: gmap(i)),
                pl.BlockSpec((g, 1, d), lambda i, j: gmap(i)),
                pl.BlockSpec((None, d, tn), lambda i, j: (layer, 0, j)),
                pl.BlockSpec((None, d, tn), lambda i, j: (layer, 0, nj + j)),
                pl.BlockSpec((FFN_CONV_W, tn), lambda i, j: (0, j)),
                pl.BlockSpec((1, tn), lambda i, j: (0, j))]
    args = [x, ng.reshape(1, d), sh, sc, w_up, w_up, cw, cb.reshape(1, f)]
    if buf is not None:
        in_specs.append(pl.BlockSpec((None, g, nh, tn), lambda i, j: (layer, i, 0, j)))
        args.append(buf)
    n_slots = nt if b == 1 else 1
    newmap = (lambda i, j: (i, 0, 0, j)) if b == 1 else (lambda i, j: (0, i, 0, j))
    return pl.pallas_call(
        functools.partial(_ffn_body, g=g, r=r, has_buf=buf is not None),
        grid=(nt, nj),
        in_specs=in_specs,
        out_specs=[pl.BlockSpec((g * r, tn), lambda i, j: (i, j)),
                   pl.BlockSpec((None, g, nh, tn), newmap)],
        out_shape=[jax.ShapeDtypeStruct((b * l, f), BF16),
                   jax.ShapeDtypeStruct((n_slots, b, nh, f), F32)],
        scratch_shapes=[pltpu.VMEM((g * r, d), BF16),
                        pltpu.VMEM((g, FFN_HALO + r, tn), F32),
                        pltpu.VMEM((nj, g, nh, tn), F32)],
        compiler_params=_cparams(2),
        name="ffn_up",
    )(*args)


def _final_body(x_ref, g_ref, o_ref):
    x = x_ref[...]
    o_ref[...] = x * lax.rsqrt(jnp.mean(x * x, axis=-1, keepdims=True) + EPS) * g_ref[...]


def _final_norm(x, g, tm=512):
    b, l, d = x.shape
    gg, r, nt, xmap = _row_tiling(b, l, tm)
    return pl.pallas_call(
        _final_body,
        grid=(nt,),
        in_specs=[pl.BlockSpec((gg, r, d), xmap), pl.BlockSpec((1, d), lambda i: (0, 0))],
        out_specs=pl.BlockSpec((gg, r, d), xmap),
        out_shape=jax.ShapeDtypeStruct((b, l, d), F32),
        compiler_params=_cparams(1),
        name="final_norm",
    )(x, g.reshape(1, d))


def _cast_body(x_ref, o_ref):
    o_ref[...] = x_ref[...].astype(o_ref.dtype)


def _cast_stack(w, tr=512):
    depth, r, c = w.shape
    spec = pl.BlockSpec((1, tr, c), lambda l, i: (l, i, 0))
    return pl.pallas_call(
        _cast_body, grid=(depth, r // tr), in_specs=[spec], out_specs=spec,
        out_shape=jax.ShapeDtypeStruct(w.shape, BF16), compiler_params=_cparams(2), name="cast_w",
    )(w)


def _prep_up_body(x_ref, o_ref):
    f = x_ref.shape[2]
    o_ref[0, :, :f] = x_ref[0].astype(BF16)
    o_ref[0, :, f:] = jnp.zeros((o_ref.shape[1], o_ref.shape[2] - f), BF16)


def _prep_w_up(w_up, f_pad, tr=256):
    depth, d, n = w_up.shape
    d_ff = n // 2
    return pl.pallas_call(
        _prep_up_body, grid=(depth, d // tr, 2),
        in_specs=[pl.BlockSpec((1, tr, d_ff), lambda l, i, h: (l, i, h))],
        out_specs=pl.BlockSpec((1, tr, f_pad), lambda l, i, h: (l, i, h)),
        out_shape=jax.ShapeDtypeStruct((depth, d, 2 * f_pad), BF16),
        compiler_params=_cparams(3), name="prep_w_up",
    )(w_up)


def _prep_down_body(x_ref, o_ref):
    f = x_ref.shape[1]
    o_ref[0, :f, :] = x_ref[0].astype(BF16)
    o_ref[0, f:, :] = jnp.zeros((o_ref.shape[1] - f, o_ref.shape[2]), BF16)


def _prep_w_down(w_down, f_pad, tc=256):
    depth, d_ff, d = w_down.shape
    return pl.pallas_call(
        _prep_down_body, grid=(depth, d // tc),
        in_specs=[pl.BlockSpec((1, d_ff, tc), lambda l, j: (l, 0, j))],
        out_specs=pl.BlockSpec((1, f_pad, tc), lambda l, j: (l, 0, j)),
        out_shape=jax.ShapeDtypeStruct((depth, f_pad, d), BF16),
        compiler_params=_cparams(2), name="prep_w_down",
    )(w_down)


PREP_TR = 1024


def _prep_in_body(main_ref, extra_ref, o_ref, *, n_gate, n_core):
    j = pl.program_id(1)

    @pl.when(j < n_gate)
    def _():
        o_ref[0] = jnp.concatenate([main_ref[0, GLA_RANK:, :], extra_ref[0]], axis=0).astype(BF16)

    @pl.when((j >= n_gate) & (j < n_gate + n_core))
    def _():
        o_ref[0] = main_ref[0].astype(BF16)

    @pl.when(j == n_gate + n_core)
    def _():
        row = lax.broadcasted_iota(jnp.int32, main_ref.shape[1:], 0)
        o_ref[0] = jnp.where(row < GLA_RANK, main_ref[0], 0.0).astype(BF16)


def _prep_w_in(w_in_t, off):
    depth, n_in, d = w_in_t.shape
    core = off["glr"] - off["rq"]
    assert core % PREP_TR == 0 and (4 * d) % PREP_TR == 0 and n_in == core + GLA_RANK + 4 * d
    n_gate, n_core = 4 * d // PREP_TR, core // PREP_TR
    c0 = core // PREP_TR

    def main_map(l, j):
        return (l, jnp.where(j < n_gate, c0 + j, jnp.where(j < n_gate + n_core, j - n_gate, c0)), 0)

    def extra_map(l, j):
        return (l, jnp.where(j < n_gate, (c0 + j + 1) * (PREP_TR // GLA_RANK), 0), 0)

    return pl.pallas_call(
        functools.partial(_prep_in_body, n_gate=n_gate, n_core=n_core),
        grid=(depth, n_gate + n_core + 1),
        in_specs=[pl.BlockSpec((1, PREP_TR, d), main_map), pl.BlockSpec((1, GLA_RANK, d), extra_map)],
        out_specs=pl.BlockSpec((1, PREP_TR, d), lambda l, j: (l, j, 0)),
        out_shape=jax.ShapeDtypeStruct((depth, off["total"], d), BF16),
        compiler_params=_cparams(2), name="prep_w_in",
    )(w_in_t, w_in_t)


def _pad_cols(a, n):
    return jnp.pad(a, [(0, 0)] * (a.ndim - 1) + [(0, n - a.shape[-1])])


def _layer(x, mods, cos, sin, layer, states, sb_cache, sw, lw, off):
    b, l, d = x.shape
    sh1, sc1, g1, sh2, sc2, g2 = mods
    ret_s0, conv_buf, gla_s0t, ffn_buf = states if states is not None else (None,) * 4
    proj = _modproj(x, lw["norm1_g"], sh1, sc1, sw["w_in"], layer, tm=1024, tn=768)

    o_a, ret_new = _retention(proj, off, b, l, cos, sin, ret_s0, layer)
    if sb_cache is None:
        o_b = _sb_prompt(proj, off, l, lw["sb_bias"])
    else:
        cache_k, cache_v, page_table = sb_cache
        o_b = _sb_sample(proj, off, b, l, lw["sb_bias"], cache_k, cache_v, layer, page_table)
    o_c, conv_new = _conv_branch(proj, off, b, l, conv_buf, layer, lw["conv_w"], lw["conv_b"],
                                 lw["conv_ln_g"], lw["conv_ln_b"])
    o_d, gla_t = _gla(proj, off, b, l, lw["gla_w_lr"], lw["gla_b_lr"], gla_s0t, layer)
    gla_new = gla_t.reshape(b, HEAD_DIM, N_HEADS, GLA_DK).transpose(0, 2, 3, 1)

    merged = _merge((o_a, o_b, o_c, o_d), proj, sw["w_branch"], layer)
    x1 = _resid_proj(merged, sw["w_out"], layer, x, g1)

    y, ffn_slots = _ffn_up(x1, lw["norm2_g"], sh2, sc2, sw["w_up"], layer,
                           lw["ffn_conv_w"], lw["ffn_conv_b"], ffn_buf)
    x2 = _resid_proj(y, sw["w_down"], layer, x1, g2)

    bw = d // 4
    sk = proj[:, off["sk"]:off["sk"] + bw].reshape(b, l, N_HEADS, HEAD_DIM)
    sv = proj[:, off["sv"]:off["sv"] + bw].reshape(b, l, N_HEADS, HEAD_DIM)
    return x2, (sk, sv, ret_new, conv_new, gla_new, ffn_slots[-1])


def kernel(x_prompt, x_sample, cache_sb_k, cache_sb_v, page_table, state_ret, state_conv, state_gla,
           state_ffn_conv, c_prompt, c_sample, norm1_g, norm2_g, w_ada, b_ada, w_in, gla_w_lr, gla_b_lr,
           sb_bias, conv_w, conv_b, conv_ln_g, conv_ln_b, w_branch, w_out, w_up, ffn_conv_w, ffn_conv_b,
           w_down, final_g):
    bp, lp, d = x_prompt.shape
    bs, ls, _ = x_sample.shape
    depth = w_in.shape[0]
    d_ff = w_down.shape[1]
    f_pad = -(-d_ff // 512) * 512
    past_len = page_table.shape[1] * PAGE
    off = _layout(d)

    n_c = bp + bs
    m_pad = -(-n_c // 8) * 8
    c_all = jnp.concatenate([c_prompt, c_sample, jnp.zeros((m_pad - n_c, d), F32)], axis=0)
    mod = _ada(c_all, w_ada, b_ada)

    cos_p, sin_p = _rope_tables(jnp.arange(lp))
    cos_s, sin_s = _rope_tables(past_len + jnp.arange(ls))

    sw = {
        "w_in": _prep_w_in(jnp.swapaxes(w_in, 1, 2), off),
        "w_branch": _cast_stack(w_branch.reshape(depth, -1, d)),
        "w_out": _cast_stack(w_out),
        "w_up": _prep_w_up(w_up, f_pad),
        "w_down": _prep_w_down(w_down, f_pad),
    }
    gla_t = state_gla.transpose(0, 1, 4, 2, 3).reshape(depth, bs, HEAD_DIM, N_HEADS * GLA_DK)
    states_s = (state_ret, state_conv, gla_t, _pad_cols(state_ffn_conv, f_pad))
    sb_cache = (cache_sb_k, cache_sb_v, page_table)

    xp, xs = x_prompt, x_sample
    out_p = [[] for _ in range(6)]
    out_s = [[] for _ in range(6)]
    for l in range(depth):
        lw = {
            "norm1_g": norm1_g[l], "norm2_g": norm2_g[l],
            "gla_w_lr": jnp.pad(gla_w_lr[l], ((0, LANE - GLA_RANK), (0, 0))).astype(BF16),
            "gla_b_lr": gla_b_lr[l].reshape(1, -1),
            "sb_bias": sb_bias[l],
            "conv_w": conv_w[l], "conv_b": conv_b[l], "conv_ln_g": conv_ln_g[l], "conv_ln_b": conv_ln_b[l],
            "ffn_conv_w": _pad_cols(ffn_conv_w[l], f_pad), "ffn_conv_b": _pad_cols(ffn_conv_b[l], f_pad),
        }
        mod_l = mod[l]
        mods_p = tuple(mod_l[:bp, i * d:(i + 1) * d].reshape(bp, 1, d) for i in range(N_ADA))
        mods_s = tuple(mod_l[bp:n_c, i * d:(i + 1) * d].reshape(bs, 1, d) for i in range(N_ADA))

        xp, st_p = _layer(xp, mods_p, cos_p, sin_p, l, None, None, sw, lw, off)
        xs, st_s = _layer(xs, mods_s, cos_s, sin_s, l, states_s, sb_cache, sw, lw, off)
        for i in range(6):
            out_p[i].append(st_p[i])
            out_s[i].append(st_s[i])

    y_prompt = _final_norm(xp, final_g)
    y_sample = _final_norm(xs, final_g)
    kp, vp, rp, cp, gp, fp = [jnp.stack(t, axis=0) for t in out_p]
    ks_, vs_, rs_, cs_, gs_, fs_ = [jnp.stack(t, axis=0) for t in out_s]
    fp = fp[..., :d_ff]
    fs_ = fs_[..., :d_ff]
    return (y_prompt, y_sample, kp, vp, rp, cp, gp, fp, ks_, vs_, rs_, cs_, gs_, fs_)
```
